```python
import jax, jax.numpy as jnp
from jax import lax
import numpy as np

D_MODEL = 2048
BATCH = 4
SEQ = 8192
DEPTH = 1

LRU_WIDTH = 2048
LRU_BLOCKS = 16
LRU_BLOCK_W = LRU_WIDTH // LRU_BLOCKS
CONV_WIDTH = 4
LRU_C = 8.0
N_HEADS = 16
N_KV_GROUPS = 4
HEADS_PER_GROUP = N_HEADS // N_KV_GROUPS
HEAD_DIM = 128
CMP_STRIDE = 16
CMP_BLOCK = 2 * CMP_STRIDE
CMP_HIDDEN = 256
SLC_BLOCK = 64
N_SELECT = 16
WINDOW = 512
Q_BLOCK = 128
SLC_CHUNK = 16
D_FF = 4 * D_MODEL
PLE_DIM = 256
EPS = 1e-6
NEG = -1e30
FORCED_SCORE = 1e4

Q_W = N_HEADS * HEAD_DIM
KV_W = N_KV_GROUPS * HEAD_DIM
N_NSA_GATES = 3 * N_HEADS
IN_SPLITS = (LRU_WIDTH, LRU_WIDTH, Q_W, 6 * KV_W, N_NSA_GATES, D_MODEL, D_MODEL)
D_IN = sum(IN_SPLITS)

kernel_name = "hybrid_rglru_nsa_gated_block"


def rms_norm(x, g):
    xf = x.astype(jnp.float32)
    y = xf * lax.rsqrt(jnp.mean(xf * xf, axis=-1, keepdims=True) + EPS)
    return (y * g.astype(jnp.float32)).astype(x.dtype)


def masked_softmax(s, mask):
    s = jnp.where(mask, s.astype(jnp.float32), NEG)
    m = jnp.max(s, axis=-1, keepdims=True)
    e = jnp.where(mask, jnp.exp(s - m), 0.0)
    return e / jnp.maximum(jnp.sum(e, axis=-1, keepdims=True), 1e-30)


def causal_depthwise_conv(x, w, b):
    y = lax.conv_general_dilated(x, w[:, None, :].astype(x.dtype), window_strides=(1,),
                                 padding=[(CONV_WIDTH - 1, 0)],
                                 dimension_numbers=('NWC', 'WIO', 'NWC'),
                                 feature_group_count=x.shape[-1])
    return y + b


def block_diag_linear(x, w, b):
    B_, S_, _ = x.shape
    xb = x.reshape(B_, S_, LRU_BLOCKS, LRU_BLOCK_W)
    return (jnp.einsum('bsnc,ncd->bsnd', xb, w) + b).reshape(B_, S_, LRU_WIDTH)


def rg_lru(x, w_a, b_a, w_x, b_x, lam):
    r = jax.nn.sigmoid(block_diag_linear(x, w_a, b_a)).astype(jnp.float32)
    i = jax.nn.sigmoid(block_diag_linear(x, w_x, b_x))
    log_a = -LRU_C * r * jax.nn.softplus(-lam.astype(jnp.float32))
    a = jnp.exp(log_a)
    mult = jnp.sqrt(jnp.maximum(-jnp.expm1(2.0 * log_a), 0.0))
    bt = mult * (i * x).astype(jnp.float32)

    def combine(left, right):
        a_l, b_l = left
        a_r, b_r = right
        return a_l * a_r, a_r * b_l + b_r

    _, h = lax.associative_scan(combine, (a, bt), axis=1)
    return h.astype(x.dtype)


def compress_tokens(k, pos, w1, w2):
    B_, S_ = k.shape[:2]
    kr = k.reshape(B_, S_ // CMP_STRIDE, CMP_STRIDE, N_KV_GROUPS, HEAD_DIM)
    blocks = jnp.concatenate([kr[:, :-1], kr[:, 1:]], axis=2)
    blocks = blocks + pos[None, None, :, None, :]
    flat = jnp.moveaxis(blocks, 3, 2).reshape(B_, -1, N_KV_GROUPS, CMP_BLOCK * HEAD_DIM)
    return jax.nn.gelu(flat @ w1) @ w2


def nsa_attention(q, k_cmp, v_cmp, k_slc, v_slc, k_win, v_win, gates,
                  cmp_pos_k, cmp_w1_k, cmp_w2_k, cmp_pos_v, cmp_w1_v, cmp_w2_v):
    B_, S_ = q.shape[:2]
    G, HPG, DK = N_KV_GROUPS, HEADS_PER_GROUP, HEAD_DIM
    q = q.reshape(B_, S_, G, HPG, DK) * (DK ** -0.5)
    kc = compress_tokens(k_cmp, cmp_pos_k, cmp_w1_k, cmp_w2_k)
    vc = compress_tokens(v_cmp, cmp_pos_v, cmp_w1_v, cmp_w2_v)
    n_cmp = S_ // CMP_STRIDE - 1
    n_sb = S_ // SLC_BLOCK
    n_sel = min(N_SELECT, n_sb)
    c_start = jnp.arange(n_cmp) * CMP_STRIDE
    cmp_end = c_start + CMP_BLOCK - 1
    s_start = jnp.arange(n_sb) * SLC_BLOCK
    overlap = ((c_start[:, None] < s_start[None, :] + SLC_BLOCK) &
               (s_start[None, :] < c_start[:, None] + CMP_BLOCK)).astype(jnp.float32)
    ks_blocks = jnp.moveaxis(k_slc.reshape(B_, n_sb, SLC_BLOCK, G, DK), 3, 1)
    vs_blocks = jnp.moveaxis(v_slc.reshape(B_, n_sb, SLC_BLOCK, G, DK), 3, 1)
    pad = ((0, 0), (WINDOW, 0), (0, 0), (0, 0))
    kw = jnp.pad(k_win, pad)
    vw = jnp.pad(v_win, pad)
    span = WINDOW + Q_BLOCK
    n_chunk = Q_BLOCK // SLC_CHUNK
    blk_j = jnp.arange(n_sb)

    def selected_attention(qb, sel, t):
        q_c = jnp.moveaxis(qb.reshape(B_, n_chunk, SLC_CHUNK, G, HPG, DK), 1, 0)
        sel_c = jnp.moveaxis(sel.reshape(B_, G, n_chunk, SLC_CHUNK, n_sel), 2, 0)
        t_c = t.reshape(n_chunk, SLC_CHUNK)

        def chunk(args):
            qc, ic, tc = args
            gather = jax.vmap(jax.vmap(lambda kb, ib: kb[ib]))
            kg = gather(ks_blocks, ic).reshape(B_, G, SLC_CHUNK, n_sel * SLC_BLOCK, DK)
            vg = gather(vs_blocks, ic).reshape(B_, G, SLC_CHUNK, n_sel * SLC_BLOCK, DK)
            pos = (ic[..., None] * SLC_BLOCK + jnp.arange(SLC_BLOCK)).reshape(B_, G, SLC_CHUNK, n_sel * SLC_BLOCK)
            s = jnp.einsum('bcghd,bgckd->bghck', qc, kg)
            mask = (pos <= tc[None, None, :, None])[:, :, None]
            pr = masked_softmax(s, mask)
            return jnp.einsum('bghck,bgckd->bcghd', pr.astype(vg.dtype), vg)

        o = lax.map(chunk, (q_c, sel_c, t_c))
        return jnp.moveaxis(o, 0, 1).reshape(B_, Q_BLOCK, G, HPG, DK)

    def attend_block(blk):
        s0 = blk * Q_BLOCK
        t = s0 + jnp.arange(Q_BLOCK)
        qb = lax.dynamic_slice_in_dim(q, s0, Q_BLOCK, axis=1)
        sc = jnp.einsum('bqghd,bcgd->bghqc', qb, kc)
        pc = masked_softmax(sc, cmp_end[None, :] <= t[:, None])
        o_cmp = jnp.einsum('bghqc,bcgd->bqghd', pc.astype(vc.dtype), vc)
        imp = jnp.einsum('bghqc,cn->bgqn', pc, overlap)
        cur = t // SLC_BLOCK
        forced = (blk_j[None, :] == 0) | (blk_j[None, :] == cur[:, None]) | (blk_j[None, :] == cur[:, None] - 1)
        valid = blk_j[None, :] <= cur[:, None]
        imp = jnp.where(forced, FORCED_SCORE, jnp.where(valid, imp, -FORCED_SCORE))
        _, sel = lax.top_k(imp, n_sel)
        o_slc = selected_attention(qb, sel, t)
        kwb = lax.dynamic_slice_in_dim(kw, s0, span, axis=1)
        vwb = lax.dynamic_slice_in_dim(vw, s0, span, axis=1)
        pos_w = s0 - WINDOW + jnp.arange(span)
        diff = t[:, None] - pos_w[None, :]
        mask_w = (diff >= 0) & (diff < WINDOW) & (pos_w[None, :] >= 0)
        sw = jnp.einsum('bqghd,bkgd->bghqk', qb, kwb)
        pw = masked_softmax(sw, mask_w)
        o_win = jnp.einsum('bghqk,bkgd->bqghd', pw.astype(vwb.dtype), vwb)
        gb = lax.dynamic_slice_in_dim(gates, s0, Q_BLOCK, axis=1)
        return (gb[:, :, 0, :, :, None] * o_cmp + gb[:, :, 1, :, :, None] * o_slc
                + gb[:, :, 2, :, :, None] * o_win)

    o = lax.map(attend_block, jnp.arange(S_ // Q_BLOCK))
    return jnp.moveaxis(o, 0, 1).reshape(B_, S_, Q_W)


def setup_inputs(seed: int = 0) -> dict:
    key = jax.random.key(seed)
    ks = jax.random.split(key, 32)
    f32 = jnp.float32

    def nrm(k, shape, scale):
        return jax.random.normal(k, shape, f32) * scale

    def gain(k, shape):
        return 1.0 + 0.01 * jax.random.normal(k, shape, f32)

    L = DEPTH
    u = jax.random.uniform(ks[8], (L, LRU_WIDTH), f32, 0.9, 0.999)
    a0 = u ** (1.0 / LRU_C)
    lru_lambda = jnp.log(a0) - jnp.log1p(-a0)
    return {
        "x": nrm(ks[0], (BATCH, SEQ, D_MODEL), 1.0),
        "p": nrm(ks[1], (DEPTH, BATCH, SEQ, PLE_DIM), 1.0),
        "norm_mix_g": gain(ks[2], (L, D_MODEL)),
        "w_in": nrm(ks[3], (L, D_MODEL, D_IN), D_MODEL ** -0.5),
        "b_in": nrm(ks[4], (L, D_IN), 0.01),
        "conv_w": nrm(ks[5], (L, CONV_WIDTH, LRU_WIDTH), CONV_WIDTH ** -0.5),
        "conv_b": nrm(ks[6], (L, LRU_WIDTH), 0.01),
        "w_gate_a": nrm(ks[7], (L, LRU_BLOCKS, LRU_BLOCK_W, LRU_BLOCK_W), LRU_BLOCK_W ** -0.5),
        "b_gate_a": nrm(ks[9], (L, LRU_BLOCKS, LRU_BLOCK_W), 0.01),
        "w_gate_x": nrm(ks[10], (L, LRU_BLOCKS, LRU_BLOCK_W, LRU_BLOCK_W), LRU_BLOCK_W ** -0.5),
        "b_gate_x": nrm(ks[11], (L, LRU_BLOCKS, LRU_BLOCK_W), 0.01),
        "lru_lambda": lru_lambda,
        "w_lru_out": nrm(ks[12], (L, LRU_WIDTH, D_MODEL), LRU_WIDTH ** -0.5),
        "cmp_pos_k": nrm(ks[13], (L, CMP_BLOCK, HEAD_DIM), 0.1),
        "cmp_w1_k": nrm(ks[14], (L, CMP_BLOCK * HEAD_DIM, CMP_HIDDEN), (CMP_BLOCK * HEAD_DIM) ** -0.5),
        "cmp_w2_k": nrm(ks[15], (L, CMP_HIDDEN, HEAD_DIM), CMP_HIDDEN ** -0.5),
        "cmp_pos_v": nrm(ks[16], (L, CMP_BLOCK, HEAD_DIM), 0.1),
        "cmp_w1_v": nrm(ks[17], (L, CMP_BLOCK * HEAD_DIM, CMP_HIDDEN), (CMP_BLOCK * HEAD_DIM) ** -0.5),
        "cmp_w2_v": nrm(ks[18], (L, CMP_HIDDEN, HEAD_DIM), CMP_HIDDEN ** -0.5),
        "w_attn_out": nrm(ks[19], (L, Q_W, D_MODEL), Q_W ** -0.5),
        "w_o": nrm(ks[20], (L, D_MODEL, D_MODEL), D_MODEL ** -0.5),
        "norm_mlp_g": gain(ks[21], (L, D_MODEL)),
        "w_up": nrm(ks[22], (L, D_MODEL, D_FF), D_MODEL ** -0.5),
        "w_down": nrm(ks[23], (L, D_FF, D_MODEL), D_FF ** -0.5),
        "norm_ple_g": gain(ks[24], (L, D_MODEL)),
        "w_ple_gate": nrm(ks[25], (L, D_MODEL, D_MODEL), D_MODEL ** -0.5),
        "w_ple_proj": nrm(ks[26], (L, PLE_DIM, D_MODEL), PLE_DIM ** -0.5),
        "norm_final_g": gain(ks[27], (D_MODEL,)),
    }


def reference(x, p, norm_mix_g, w_in, b_in, conv_w, conv_b, w_gate_a, b_gate_a, w_gate_x, b_gate_x,
              lru_lambda, w_lru_out, cmp_pos_k, cmp_w1_k, cmp_w2_k, cmp_pos_v, cmp_w1_v, cmp_w2_v,
              w_attn_out, w_o, norm_mlp_g, w_up, w_down, norm_ple_g, w_ple_gate, w_ple_proj,
              norm_final_g):
    B_, S_, _ = x.shape
    split_points = [int(v) for v in np.cumsum(IN_SPLITS)[:-1]]
    for i in range(DEPTH):
        h = rms_norm(x, norm_mix_g[i])
        u = h @ w_in[i] + b_in[i]
        u_lx, u_ly, u_q, u_kv, u_g, u_ma, u_mb = jnp.split(u, split_points, axis=-1)
        xc = causal_depthwise_conv(u_lx, conv_w[i], conv_b[i])
        hl = rg_lru(xc, w_gate_a[i], b_gate_a[i], w_gate_x[i], b_gate_x[i], lru_lambda[i])
        y_a = (jax.nn.gelu(u_ly) * hl) @ w_lru_out[i]
        k_cmp, v_cmp, k_slc, v_slc, k_win, v_win = [
            t.reshape(B_, S_, N_KV_GROUPS, HEAD_DIM) for t in jnp.split(u_kv, 6, axis=-1)]
        gates = jax.nn.sigmoid(u_g).reshape(B_, S_, 3, N_KV_GROUPS, HEADS_PER_GROUP)
        o_nsa = nsa_attention(u_q, k_cmp, v_cmp, k_slc, v_slc, k_win, v_win, gates,
                              cmp_pos_k[i], cmp_w1_k[i], cmp_w2_k[i],
                              cmp_pos_v[i], cmp_w1_v[i], cmp_w2_v[i])
        y_b = o_nsa @ w_attn_out[i]
        mixed = jax.nn.sigmoid(u_ma) * y_a + jax.nn.sigmoid(u_mb) * y_b
        x = x + mixed @ w_o[i]
        h = rms_norm(x, norm_mlp_g[i])
        x = x + jnp.square(jax.nn.relu(h @ w_up[i])) @ w_down[i]
        h = rms_norm(x, norm_ple_g[i])
        x = x + jax.nn.sigmoid(h @ w_ple_gate[i]) * (p[i] @ w_ple_proj[i])
    return rms_norm(x, norm_final_g)
```

```python
import functools

import jax
import jax.numpy as jnp
from jax import lax
from jax.experimental import pallas as pl
from jax.experimental.pallas import tpu as pltpu

F32 = jnp.float32
BF16 = jnp.bfloat16

LRU_BLOCK_W = 128
CONV_WIDTH = 4
LRU_C = 8.0
N_HEADS = 16
N_KV_GROUPS = 4
HPG = N_HEADS // N_KV_GROUPS
HEAD_DIM = 128
CMP_STRIDE = 16
CMP_BLOCK = 32
SLC_BLOCK = 64
N_SELECT = 16
WINDOW = 512
Q_BLOCK = 128
EPS = 1e-6
NEG = -1e30
FORCED_SCORE = 1e4
KNOCKED_OUT = -3e38

VMEM_LIMIT = 56 * 1024 * 1024


def _params(sem):
    return pltpu.CompilerParams(dimension_semantics=sem, vmem_limit_bytes=VMEM_LIMIT)


def _rms(xf, g):
    return xf * lax.rsqrt(jnp.mean(xf * xf, axis=-1, keepdims=True) + EPS) * g


def _gelu(x):
    return jax.nn.gelu(x)


def _mm_body(*refs, has_norm, n_extra, epi, split):
    a_ref, w_ref = refs[0], refs[1]
    idx = 2
    if has_norm:
        g_ref = refs[idx]
        idx += 1
    extras = refs[idx:idx + n_extra]
    idx += n_extra
    o_ref = refs[idx]
    idx += 1
    if has_norm:
        h_ref = refs[idx]

        @pl.when(pl.program_id(1) == 0)
        def _():
            h_ref[...] = _rms(a_ref[...].astype(F32), g_ref[...]).astype(BF16)

        lhs = h_ref[...]
    else:
        lhs = a_ref[...].astype(BF16)
    acc = jnp.dot(lhs, w_ref[...], preferred_element_type=F32)
    res = epi(acc, *[e[...] for e in extras])
    if split:
        for g in range(split):
            o_ref[g] = res[:, g * HEAD_DIM:(g + 1) * HEAD_DIM].astype(o_ref.dtype)
    else:
        o_ref[...] = res.astype(o_ref.dtype)


def _mm(a, w, epi, extras=(), *, norm_g=None, out_dtype, tm=512, tn=512, split=0, name):
    M, K = a.shape
    N = w.shape[1]
    tn = min(tn, N)
    grid = (M // tm, N // tn)
    in_specs = [pl.BlockSpec((tm, K), lambda i, j: (i, 0)),
                pl.BlockSpec((K, tn), lambda i, j: (0, j))]
    args = [a, w]
    if norm_g is not None:
        in_specs.append(pl.BlockSpec((1, K), lambda i, j: (0, 0)))
        args.append(norm_g.reshape(1, K).astype(F32))
    for arr, kind, off in extras:
        if kind == 'row':
            in_specs.append(pl.BlockSpec((1, tn), lambda i, j, off=off: (0, j + off)))
        else:
            in_specs.append(pl.BlockSpec((tm, tn), lambda i, j, off=off: (i, j + off)))
        args.append(arr)
    if split:
        out_shape = jax.ShapeDtypeStruct((N // tn, split, M, HEAD_DIM), out_dtype)
        out_spec = pl.BlockSpec((None, split, tm, HEAD_DIM), lambda i, j: (j, 0, i, 0))
    else:
        out_shape = jax.ShapeDtypeStruct((M, N), out_dtype)
        out_spec = pl.BlockSpec((tm, tn), lambda i, j: (i, j))
    scratch = [pltpu.VMEM((tm, K), BF16)] if norm_g is not None else []
    return pl.pallas_call(
        functools.partial(_mm_body, has_norm=norm_g is not None, n_extra=len(extras),
                          epi=epi, split=split),
        grid=grid, in_specs=in_specs, out_specs=out_spec, out_shape=out_shape,
        scratch_shapes=scratch,
        compiler_params=_params(("parallel", "arbitrary")),
        name=name,
    )(*args)


def _lru_body(u_ref, gy_ref, cw_ref, cb_ref, wa_ref, ba_ref, wx_ref, bx_ref, lam_ref,
              o_ref, xe_ref, a_ref, b_ref, h_ref, *, T, ct):
    s = pl.program_id(2)

    @pl.when(s == 0)
    def _():
        xe_ref[0:8, :] = jnp.zeros((8, ct), F32)
        h_ref[...] = jnp.zeros((8, ct), F32)

    xe_ref[8:, :] = u_ref[...]
    xc = cb_ref[...] + cw_ref[0:1, :] * xe_ref[pl.ds(8 - CONV_WIDTH + 1, T), :]
    for k in range(1, CONV_WIDTH):
        xc = xc + cw_ref[k:k + 1, :] * xe_ref[pl.ds(8 - CONV_WIDTH + 1 + k, T), :]
    xe_ref[0:8, :] = xe_ref[T:T + 8, :]

    lam = lam_ref[...]
    neg_softplus = -(jnp.maximum(-lam, 0.0) + jnp.log1p(jnp.exp(-jnp.abs(lam))))
    log_a_scale = LRU_C * neg_softplus
    row_in_group = lax.broadcasted_iota(jnp.int32, (T, LRU_BLOCK_W), 0) % 8
    xcb = xc.astype(BF16)
    for n in range(ct // LRU_BLOCK_W):
        sl = slice(n * LRU_BLOCK_W, (n + 1) * LRU_BLOCK_W)
        xb = xcb[:, sl]
        r = jax.nn.sigmoid(jnp.dot(xb, wa_ref[n], preferred_element_type=F32) + ba_ref[:, sl])
        i = jax.nn.sigmoid(jnp.dot(xb, wx_ref[n], preferred_element_type=F32) + bx_ref[:, sl])
        log_a = r * log_a_scale[:, sl]
        a = jnp.exp(log_a)
        mult = jnp.sqrt(jnp.maximum(1.0 - a * a, 0.0))
        b = mult * (i * xc[:, sl])
        for d in (1, 2, 4):
            m = row_in_group >= d
            a_sh = pltpu.roll(a, d, axis=0)
            b_sh = pltpu.roll(b, d, axis=0)
            b = jnp.where(m, a * b_sh + b, b)
            a = jnp.where(m, a * a_sh, a)
        a_ref[:, sl] = a
        b_ref[:, sl] = b

    def group(gi, hprev):
        r0 = pl.multiple_of(gi * 8, 8)
        ht = a_ref[pl.ds(r0, 8), :] * hprev + b_ref[pl.ds(r0, 8), :]
        b_ref[pl.ds(r0, 8), :] = ht
        return jnp.broadcast_to(ht[7:8, :], (8, ct))

    h_ref[...] = lax.fori_loop(0, T // 8, group, h_ref[...], unroll=8)
    o_ref[...] = (b_ref[...] * gy_ref[...].astype(F32)).astype(o_ref.dtype)


def _rg_lru(u_lx, gelu_y, conv_w, conv_b, w_a, b_a, w_x, b_x, lam, *, B, S, T=512, ct=512):
    N, C = u_lx.shape
    nb = ct // LRU_BLOCK_W
    ns = S // T
    row = lambda v: v.reshape(1, C).astype(F32)
    tile = pl.BlockSpec((T, ct), lambda b, c, s: (b * ns + s, c))
    vec = pl.BlockSpec((1, ct), lambda b, c, s: (0, c))
    wblk = pl.BlockSpec((nb, LRU_BLOCK_W, LRU_BLOCK_W), lambda b, c, s: (c, 0, 0))
    return pl.pallas_call(
        functools.partial(_lru_body, T=T, ct=ct),
        grid=(B, C // ct, ns),
        in_specs=[tile, tile, pl.BlockSpec((CONV_WIDTH, ct), lambda b, c, s: (0, c)), vec,
                  wblk, vec, wblk, vec, vec],
        out_specs=tile,
        out_shape=jax.ShapeDtypeStruct((N, C), BF16),
        scratch_shapes=[pltpu.VMEM((T + 8, ct), F32), pltpu.VMEM((T, ct), F32),
                        pltpu.VMEM((T, ct), F32), pltpu.VMEM((8, ct), F32)],
        compiler_params=_params(("parallel", "parallel", "arbitrary")),
        name="rg_lru",
    )(u_lx, gelu_y, conv_w.astype(F32), row(conv_b), w_a.astype(BF16), row(b_a),
      w_x.astype(BF16), row(b_x), row(lam))


def _cmp_body(x_ref, pos_ref, w1_ref, w2_ref, o_ref, *, nc, half):
    x = x_ref[...]
    first = jnp.dot(x, w1_ref[0:half, :], preferred_element_type=F32)
    second = jnp.dot(x, w1_ref[half:, :], preferred_element_type=F32)
    pos = jnp.broadcast_to(pos_ref[...], (8, 2 * half)).astype(BF16)
    cpos = jnp.dot(pos, w1_ref[...], preferred_element_type=F32)[0:1, :]
    hid = first + pltpu.roll(second, nc - 1, axis=0) + cpos
    o_ref[...] = jnp.dot(_gelu(hid).astype(BF16), w2_ref[...],
                         preferred_element_type=F32).astype(o_ref.dtype)


def _compress(x, pos, w1, w2, *, B, name):
    G, rows, half = x.shape
    nc = rows // B
    hidden = w1.shape[1]
    return pl.pallas_call(
        functools.partial(_cmp_body, nc=nc, half=half),
        grid=(B, G),
        in_specs=[pl.BlockSpec((None, nc, half), lambda b, g: (g, b, 0)),
                  pl.BlockSpec((1, 2 * half), lambda b, g: (0, 0)),
                  pl.BlockSpec((2 * half, hidden), lambda b, g: (0, 0)),
                  pl.BlockSpec((hidden, HEAD_DIM), lambda b, g: (0, 0))],
        out_specs=pl.BlockSpec((None, None, nc, HEAD_DIM), lambda b, g: (b, g, 0, 0)),
        out_shape=jax.ShapeDtypeStruct((B, G, nc, HEAD_DIM), BF16),
        compiler_params=_params(("parallel", "parallel")),
        name=name,
    )(x, pos.reshape(1, 2 * half).astype(F32), w1.astype(BF16), w2.astype(BF16))


def _qk(q, k):
    return lax.dot_general(q, k, (((1,), (1,)), ((), ())), preferred_element_type=F32)


def _nsa_body(q_ref, kc_ref, vc_ref, ks_ref, vs_ref, kw_ref, vw_ref, g_ref, o_ref,
              *, S, nc, n_sb, n_sel, tk):
    R = HPG * Q_BLOCK
    s0 = pl.program_id(2) * Q_BLOCK
    qb = q_ref[...]
    q4 = jnp.concatenate([qb[:, h * HEAD_DIM:(h + 1) * HEAD_DIM] for h in range(HPG)], axis=0)
    tq = s0 + lax.broadcasted_iota(jnp.int32, (Q_BLOCK, 1), 0)

    def softmax3(s3, ok):
        s3 = jnp.where(ok[None], s3, NEG)
        m = jnp.max(s3, axis=-1, keepdims=True)
        e = jnp.where(ok[None], jnp.exp(s3 - m), 0.0)
        return e / jnp.maximum(jnp.sum(e, axis=-1, keepdims=True), 1e-30)

    cend = lax.broadcasted_iota(jnp.int32, (1, nc), 1) * CMP_STRIDE + (CMP_BLOCK - 1)
    pc = softmax3(_qk(q4, kc_ref[...]).reshape(HPG, Q_BLOCK, nc), cend <= tq)
    o_cmp = jnp.dot(pc.reshape(R, nc).astype(BF16), vc_ref[...], preferred_element_type=F32)

    pc_sum = pc[0]
    for h in range(1, HPG):
        pc_sum = pc_sum + pc[h]
    ci = lax.broadcasted_iota(jnp.int32, (nc, n_sb), 0) * CMP_STRIDE
    si = lax.broadcasted_iota(jnp.int32, (nc, n_sb), 1) * SLC_BLOCK
    overlap = ((ci < si + SLC_BLOCK) & (si < ci + CMP_BLOCK)).astype(F32)
    imp = jnp.dot(pc_sum, overlap, preferred_element_type=F32, precision=lax.Precision.HIGHEST)
    blk = lax.broadcasted_iota(jnp.int32, (1, n_sb), 1)
    cur = tq // SLC_BLOCK
    forced = (blk == 0) | (blk == cur) | (blk == cur - 1)
    score = jnp.where(forced, FORCED_SCORE, jnp.where(blk <= cur, imp, -FORCED_SCORE))
    blk_f = blk.astype(F32)
    sel = jnp.zeros((Q_BLOCK, n_sb), F32)
    for _ in range(n_sel):
        mx = jnp.max(score, axis=-1, keepdims=True)
        first = jnp.min(jnp.where(score == mx, blk_f, float(n_sb)), axis=-1, keepdims=True)
        hit = blk_f == first
        sel = jnp.where(hit, 1.0, sel)
        score = jnp.where(hit, KNOCKED_OUT, score)
    sel_b = sel.astype(BF16)

    blk_col = lax.broadcasted_iota(jnp.int32, (n_sb, tk), 0)
    key_off = lax.broadcasted_iota(jnp.int32, (n_sb, tk), 1)
    key_row = lax.broadcasted_iota(jnp.int32, (1, tk), 1)

    def slc_tile(kt, carry):
        m_i, l_i, acc = carry
        k0 = pl.multiple_of(kt * tk, tk)
        s3 = _qk(q4, ks_ref[pl.ds(k0, tk), :]).reshape(HPG, Q_BLOCK, tk)
        expand = (blk_col == (k0 + key_off) // SLC_BLOCK).astype(BF16)
        chosen = jnp.dot(sel_b, expand, preferred_element_type=F32)
        ok = (chosen > 0.5) & (k0 + key_row <= tq)
        s3 = jnp.where(ok[None], s3, NEG)
        m_new = jnp.maximum(m_i, jnp.max(s3, axis=-1, keepdims=True))
        p = jnp.where(ok[None], jnp.exp(s3 - m_new), 0.0)
        alpha = jnp.exp(m_i - m_new)
        l_new = alpha * l_i + jnp.sum(p, axis=-1, keepdims=True)
        pv = jnp.dot(p.reshape(R, tk).astype(BF16), vs_ref[pl.ds(k0, tk), :],
                     preferred_element_type=F32)
        return m_new, l_new, alpha * acc + pv.reshape(HPG, Q_BLOCK, HEAD_DIM)

    n_tiles = (s0 + Q_BLOCK + tk - 1) // tk
    init = (jnp.full((HPG, Q_BLOCK, 1), NEG, F32), jnp.zeros((HPG, Q_BLOCK, 1), F32),
            jnp.zeros((HPG, Q_BLOCK, HEAD_DIM), F32))
    _, l_s, acc_s = lax.fori_loop(0, n_tiles, slc_tile, init)
    o_slc = acc_s / jnp.maximum(l_s, 1e-30)

    span = WINDOW + Q_BLOCK
    w0 = pl.multiple_of(jnp.maximum(s0 - WINDOW, 0), Q_BLOCK)
    pos_w = w0 + lax.broadcasted_iota(jnp.int32, (1, span), 1)
    diff = tq - pos_w
    pw = softmax3(_qk(q4, kw_ref[pl.ds(w0, span), :]).reshape(HPG, Q_BLOCK, span),
                  (diff >= 0) & (diff < WINDOW))
    o_win = jnp.dot(pw.reshape(R, span).astype(BF16), vw_ref[pl.ds(w0, span), :],
                    preferred_element_type=F32)

    gt = g_ref[...]
    o_cmp = o_cmp.reshape(HPG, Q_BLOCK, HEAD_DIM)
    o_win = o_win.reshape(HPG, Q_BLOCK, HEAD_DIM)
    for h in range(HPG):
        o = (gt[:, h:h + 1] * o_cmp[h] + gt[:, HPG + h:HPG + h + 1] * o_slc[h]
             + gt[:, 2 * HPG + h:2 * HPG + h + 1] * o_win[h])
        o_ref[:, h * HEAD_DIM:(h + 1) * HEAD_DIM] = o.astype(o_ref.dtype)


def _nsa(q, kc, vc, kv6, gates, *, B, S, tk=512):
    N = B * S
    nq = S // Q_BLOCK
    nc = kc.shape[2]
    n_sb = S // SLC_BLOCK
    G = N_KV_GROUPS
    gw = HPG * HEAD_DIM
    kv_spec = lambda t: pl.BlockSpec((None, None, S, HEAD_DIM), lambda b, g, i, t=t: (t, g, b, 0))
    cmp_spec = pl.BlockSpec((None, None, nc, HEAD_DIM), lambda b, g, i: (b, g, 0, 0))
    return pl.pallas_call(
        functools.partial(_nsa_body, S=S, nc=nc, n_sb=n_sb, n_sel=min(N_SELECT, n_sb), tk=tk),
        grid=(B, G, nq),
        in_specs=[pl.BlockSpec((Q_BLOCK, gw), lambda b, g, i: (b * nq + i, g)),
                  cmp_spec, cmp_spec, kv_spec(2), kv_spec(3), kv_spec(4), kv_spec(5),
                  pl.BlockSpec((Q_BLOCK, 128), lambda b, g, i: (b * nq + i, g))],
        out_specs=pl.BlockSpec((Q_BLOCK, gw), lambda b, g, i: (b * nq + i, g)),
        out_shape=jax.ShapeDtypeStruct((N, G * gw), BF16),
        compiler_params=_params(("parallel", "parallel", "arbitrary")),
        name="nsa_attention",
    )(q, kc, vc, kv6, kv6, kv6, kv6, gates)


def _mlp_body(x_ref, g_ref, wu_ref, wd_ref, o_ref, h_ref, acc_ref):
    f = pl.program_id(1)

    @pl.when(f == 0)
    def _():
        h_ref[...] = _rms(x_ref[...], g_ref[...]).astype(BF16)
        acc_ref[...] = jnp.zeros_like(acc_ref)

    up = jnp.dot(h_ref[...], wu_ref[...], preferred_element_type=F32)
    up = jnp.square(jnp.maximum(up, 0.0)).astype(BF16)
    acc_ref[...] += jnp.dot(up, wd_ref[...], preferred_element_type=F32)

    @pl.when(f == pl.num_programs(1) - 1)
    def _():
        o_ref[...] = x_ref[...] + acc_ref[...]


def _mlp(x, g, w_up, w_down, *, tm=512, tf=512):
    M, D = x.shape
    FF = w_up.shape[1]
    return pl.pallas_call(
        _mlp_body,
        grid=(M // tm, FF // tf),
        in_specs=[pl.BlockSpec((tm, D), lambda i, f: (i, 0)),
                  pl.BlockSpec((1, D), lambda i, f: (0, 0)),
                  pl.BlockSpec((D, tf), lambda i, f: (0, f)),
                  pl.BlockSpec((tf, D), lambda i, f: (f, 0))],
        out_specs=pl.BlockSpec((tm, D), lambda i, f: (i, 0)),
        out_shape=jax.ShapeDtypeStruct((M, D), F32),
        scratch_shapes=[pltpu.VMEM((tm, D), BF16), pltpu.VMEM((tm, D), F32)],
        compiler_params=_params(("parallel", "arbitrary")),
        name="mlp",
    )(x, g.reshape(1, D).astype(F32), w_up, w_down)


def _ple_body(x_ref, p_ref, g_ref, wg_ref, wp_ref, gf_ref, o_ref, *, final_norm):
    x = x_ref[...]
    h = _rms(x, g_ref[...]).astype(BF16)
    gate = jax.nn.sigmoid(jnp.dot(h, wg_ref[...], preferred_element_type=F32))
    proj = jnp.dot(p_ref[...].astype(BF16), wp_ref[...], preferred_element_type=F32)
    y = x + gate * proj
    o_ref[...] = _rms(y, gf_ref[...]) if final_norm else y


def _ple(x, p, g, w_gate, w_proj, g_final, *, final_norm, tm=512):
    M, D = x.shape
    P = p.shape[1]
    row = lambda v: v.reshape(1, D).astype(F32)
    vec = pl.BlockSpec((1, D), lambda i: (0, 0))
    return pl.pallas_call(
        functools.partial(_ple_body, final_norm=final_norm),
        grid=(M // tm,),
        in_specs=[pl.BlockSpec((tm, D), lambda i: (i, 0)), pl.BlockSpec((tm, P), lambda i: (i, 0)),
                  vec, pl.BlockSpec((D, D), lambda i: (0, 0)), pl.BlockSpec((P, D), lambda i: (0, 0)),
                  vec],
        out_specs=pl.BlockSpec((tm, D), lambda i: (i, 0)),
        out_shape=jax.ShapeDtypeStruct((M, D), F32),
        compiler_params=_params(("parallel",)),
        name="ple_final",
    )(x, p, row(g), w_gate, w_proj, row(g_final))


def _layer(x, p, norm_mix_g, w_in, b_in, conv_w, conv_b, w_gate_a, b_gate_a, w_gate_x, b_gate_x,
           lru_lambda, w_lru_out, cmp_pos_k, cmp_w1_k, cmp_w2_k, cmp_pos_v, cmp_w1_v, cmp_w2_v,
           w_attn_out, w_o, norm_mlp_g, w_up, w_down, norm_ple_g, w_ple_gate, w_ple_proj,
           norm_final_g, *, B, S, final_norm):
    N, D = x.shape
    C = conv_w.shape[1]
    G = N_KV_GROUPS
    QW = N_HEADS * HEAD_DIM
    KVW = G * HEAD_DIM
    o_ly, o_q, o_kv, o_g, o_m = C, 2 * C, 2 * C + QW, 2 * C + QW + 6 * KVW, 2 * C + QW + 6 * KVW + 3 * N_HEADS
    wb = lambda lo, hi: (w_in[:, lo:hi].astype(BF16), b_in[lo:hi].reshape(1, hi - lo).astype(F32))
    proj = functools.partial(_mm, x, norm_g=norm_mix_g)

    w, b = wb(0, o_ly)
    u_lx = proj(w, lambda acc, b: acc + b, [(b, 'row', 0)], out_dtype=F32, name="in_lru_x")
    w, b = wb(o_ly, o_q)
    gelu_y = proj(w, lambda acc, b: _gelu(acc + b), [(b, 'row', 0)], out_dtype=BF16, name="in_lru_y")
    w, b = wb(o_q, o_kv)
    q = proj(w, lambda acc, b: (acc + b) * (HEAD_DIM ** -0.5), [(b, 'row', 0)], out_dtype=BF16,
             name="in_q")
    w, b = wb(o_kv, o_g)
    kv6 = proj(w, lambda acc, b: acc + b, [(b, 'row', 0)], out_dtype=BF16, tn=KVW, split=G,
               name="in_kv")
    wg = w_in[:, o_g:o_m].reshape(D, 3, G, HPG).transpose(0, 2, 1, 3).reshape(D, G, 3 * HPG)
    wg = jnp.pad(wg, ((0, 0), (0, 0), (0, 128 - 3 * HPG))).reshape(D, G * 128).astype(BF16)
    bg = b_in[o_g:o_m].reshape(3, G, HPG).transpose(1, 0, 2).reshape(G, 3 * HPG)
    bg = jnp.pad(bg, ((0, 0), (0, 128 - 3 * HPG))).reshape(1, G * 128).astype(F32)
    gates = proj(wg, lambda acc, b: jax.nn.sigmoid(acc + b), [(bg, 'row', 0)], out_dtype=F32,
                 name="in_gates")
    w, b = wb(o_m, o_m + 2 * D)
    gm = proj(w, lambda acc, b: jax.nn.sigmoid(acc + b), [(b, 'row', 0)], out_dtype=BF16,
              name="in_merge_gates")

    a_in = _rg_lru(u_lx, gelu_y, conv_w, conv_b, w_gate_a, b_gate_a, w_gate_x, b_gate_x,
                   lru_lambda, B=B, S=S)
    nc = S // CMP_STRIDE
    kc = _compress(kv6[0].reshape(G, B * nc, CMP_STRIDE * HEAD_DIM), cmp_pos_k, cmp_w1_k, cmp_w2_k,
                   B=B, name="compress_k")
    vc = _compress(kv6[1].reshape(G, B * nc, CMP_STRIDE * HEAD_DIM), cmp_pos_v, cmp_w1_v, cmp_w2_v,
                   B=B, name="compress_v")
    o_nsa = _nsa(q, kc, vc, kv6, gates, B=B, S=S)

    nt = D // 512
    ya = _mm(a_in, w_lru_out.astype(BF16), lambda acc, ga: acc * ga.astype(F32),
             [(gm, 'tile', 0)], out_dtype=BF16, name="lru_out")
    mixed = _mm(o_nsa, w_attn_out.astype(BF16),
                lambda acc, ya, gb: ya.astype(F32) + acc * gb.astype(F32),
                [(ya, 'tile', 0), (gm, 'tile', nt)], out_dtype=BF16, name="attn_out_merge")
    x1 = _mm(mixed, w_o.astype(BF16), lambda acc, xr: xr + acc, [(x, 'tile', 0)],
             out_dtype=F32, name="out_proj")
    x2 = _mlp(x1, norm_mlp_g, w_up.astype(BF16), w_down.astype(BF16))
    return _ple(x2, p, norm_ple_g, w_ple_gate.astype(BF16), w_ple_proj.astype(BF16),
                norm_final_g, final_norm=final_norm)


def kernel(x, p, norm_mix_g, w_in, b_in, conv_w, conv_b, w_gate_a, b_gate_a, w_gate_x, b_gate_x, lru_lambda, w_lru_out, cmp_pos_k, cmp_w1_k, cmp_w2_k, cmp_pos_v, cmp_w1_v, cmp_w2_v, w_attn_out, w_o, norm_mlp_g, w_up, w_down, norm_ple_g, w_ple_gate, w_ple_proj, norm_final_g):
    B, S, D = x.shape
    depth = w_in.shape[0]
    assert S % 512 == 0 and S >= WINDOW + Q_BLOCK and D % 512 == 0
    xs = x.reshape(B * S, D)
    for i in range(depth):
        per_layer = [t[i] for t in (
            norm_mix_g, w_in, b_in, conv_w, conv_b, w_gate_a, b_gate_a, w_gate_x, b_gate_x,
            lru_lambda, w_lru_out, cmp_pos_k, cmp_w1_k, cmp_w2_k, cmp_pos_v, cmp_w1_v, cmp_w2_v,
            w_attn_out, w_o, norm_mlp_g, w_up, w_down, norm_ple_g, w_ple_gate, w_ple_proj)]
        xs = _layer(xs, p[i].reshape(B * S, -1), *per_layer, norm_final_g, B=B, S=S,
                    final_norm=i == depth - 1)
    return xs.reshape(B, S, D)
```

```python
import functools

import jax
import jax.numpy as jnp
from jax import lax
from jax.experimental import pallas as pl
from jax.experimental.pallas import tpu as pltpu

F32 = jnp.float32
BF16 = jnp.bfloat16

LRU_BLOCK_W = 128
CONV_WIDTH = 4
LRU_C = 8.0
N_HEADS = 16
N_KV_GROUPS = 4
HPG = N_HEADS // N_KV_GROUPS
HEAD_DIM = 128
CMP_STRIDE = 16
CMP_BLOCK = 32
SLC_BLOCK = 64
N_SELECT = 16
WINDOW = 512
Q_BLOCK = 128
EPS = 1e-6
NEG = -1e30
FORCED_SCORE = 1e4
KNOCKED_OUT = -3e38
NB = 128
LOG2E = 1.4426950408889634

VMEM_LIMIT = 56 * 1024 * 1024


def _params(sem):
    return pltpu.CompilerParams(dimension_semantics=sem, vmem_limit_bytes=VMEM_LIMIT)


def _rms(xf, g):
    return xf * lax.rsqrt(jnp.mean(xf * xf, axis=-1, keepdims=True) + EPS) * g


def _gelu(x):
    return jax.nn.gelu(x)


def _mm_body(*refs, has_norm, n_extra, epi, split):
    a_ref, w_ref = refs[0], refs[1]
    idx = 2
    if has_norm:
        g_ref = refs[idx]
        idx += 1
    extras = refs[idx:idx + n_extra]
    idx += n_extra
    o_ref = refs[idx]
    idx += 1
    if has_norm:
        h_ref = refs[idx]

        @pl.when(pl.program_id(1) == 0)
        def _():
            h_ref[...] = _rms(a_ref[...].astype(F32), g_ref[...]).astype(BF16)

        lhs = h_ref[...]
    else:
        lhs = a_ref[...].astype(BF16)
    acc = jnp.dot(lhs, w_ref[...], preferred_element_type=F32)
    res = epi(acc, *[e[...] for e in extras])
    if split:
        for g in range(split):
            o_ref[g] = res[:, g * HEAD_DIM:(g + 1) * HEAD_DIM].astype(o_ref.dtype)
    else:
        o_ref[...] = res.astype(o_ref.dtype)


def _mm(a, w, epi, extras=(), *, norm_g=None, out_dtype, tm=512, tn=512, split=0, name):
    M, K = a.shape
    N = w.shape[1]
    tn = min(tn, N)
    grid = (M // tm, N // tn)
    in_specs = [pl.BlockSpec((tm, K), lambda i, j: (i, 0)),
                pl.BlockSpec((K, tn), lambda i, j: (0, j))]
    args = [a, w]
    if norm_g is not None:
        in_specs.append(pl.BlockSpec((1, K), lambda i, j: (0, 0)))
        args.append(norm_g.reshape(1, K).astype(F32))
    for arr, kind, off in extras:
        if kind == 'row':
            in_specs.append(pl.BlockSpec((1, tn), lambda i, j, off=off: (0, j + off)))
        else:
            in_specs.append(pl.BlockSpec((tm, tn), lambda i, j, off=off: (i, j + off)))
        args.append(arr)
    if split:
        out_shape = jax.ShapeDtypeStruct((N // tn, split, M, HEAD_DIM), out_dtype)
        out_spec = pl.BlockSpec((None, split, tm, HEAD_DIM), lambda i, j: (j, 0, i, 0))
    else:
        out_shape = jax.ShapeDtypeStruct((M, N), out_dtype)
        out_spec = pl.BlockSpec((tm, tn), lambda i, j: (i, j))
    scratch = [pltpu.VMEM((tm, K), BF16)] if norm_g is not None else []
    return pl.pallas_call(
        functools.partial(_mm_body, has_norm=norm_g is not None, n_extra=len(extras),
                          epi=epi, split=split),
        grid=grid, in_specs=in_specs, out_specs=out_spec, out_shape=out_shape,
        scratch_shapes=scratch,
        compiler_params=_params(("parallel", "arbitrary")),
        name=name,
    )(*args)


def _lru_body(u_ref, gy_ref, cw_ref, cb_ref, wa_ref, ba_ref, wx_ref, bx_ref, lam_ref,
              o_ref, xe_ref, a_ref, b_ref, h_ref, *, T, ct):
    s = pl.program_id(2)

    @pl.when(s == 0)
    def _():
        xe_ref[0:8, :] = jnp.zeros((8, ct), F32)
        h_ref[...] = jnp.zeros((8, ct), F32)

    xe_ref[8:, :] = u_ref[...]
    xc = cb_ref[...] + cw_ref[0:1, :] * xe_ref[pl.ds(8 - CONV_WIDTH + 1, T), :]
    for k in range(1, CONV_WIDTH):
        xc = xc + cw_ref[k:k + 1, :] * xe_ref[pl.ds(8 - CONV_WIDTH + 1 + k, T), :]
    xe_ref[0:8, :] = xe_ref[T:T + 8, :]

    lam = lam_ref[...]
    neg_softplus = -(jnp.maximum(-lam, 0.0) + jnp.log1p(jnp.exp(-jnp.abs(lam))))
    log_a_scale = LRU_C * neg_softplus
    row_in_group = lax.broadcasted_iota(jnp.int32, (T, LRU_BLOCK_W), 0) % 8
    xcb = xc.astype(BF16)
    for n in range(ct // LRU_BLOCK_W):
        sl = slice(n * LRU_BLOCK_W, (n + 1) * LRU_BLOCK_W)
        xb = xcb[:, sl]
        r = jax.nn.sigmoid(jnp.dot(xb, wa_ref[n], preferred_element_type=F32) + ba_ref[:, sl])
        i = jax.nn.sigmoid(jnp.dot(xb, wx_ref[n], preferred_element_type=F32) + bx_ref[:, sl])
        log_a = r * log_a_scale[:, sl]
        a = jnp.exp(log_a)
        mult = jnp.sqrt(jnp.maximum(1.0 - a * a, 0.0))
        b = mult * (i * xc[:, sl])
        for d in (1, 2, 4):
            m = row_in_group >= d
            a_sh = pltpu.roll(a, d, axis=0)
            b_sh = pltpu.roll(b, d, axis=0)
            b = jnp.where(m, a * b_sh + b, b)
            a = jnp.where(m, a * a_sh, a)
        a_ref[:, sl] = a
        b_ref[:, sl] = b

    def group(gi, hprev):
        r0 = pl.multiple_of(gi * 8, 8)
        ht = a_ref[pl.ds(r0, 8), :] * hprev + b_ref[pl.ds(r0, 8), :]
        b_ref[pl.ds(r0, 8), :] = ht
        return jnp.broadcast_to(ht[7:8, :], (8, ct))

    h_ref[...] = lax.fori_loop(0, T // 8, group, h_ref[...], unroll=8)
    o_ref[...] = (b_ref[...] * gy_ref[...].astype(F32)).astype(o_ref.dtype)


def _rg_lru(u_lx, gelu_y, conv_w, conv_b, w_a, b_a, w_x, b_x, lam, *, B, S, T=512, ct=512):
    N, C = u_lx.shape
    nb = ct // LRU_BLOCK_W
    ns = S // T
    row = lambda v: v.reshape(1, C).astype(F32)
    tile = pl.BlockSpec((T, ct), lambda b, c, s: (b * ns + s, c))
    vec = pl.BlockSpec((1, ct), lambda b, c, s: (0, c))
    wblk = pl.BlockSpec((nb, LRU_BLOCK_W, LRU_BLOCK_W), lambda b, c, s: (c, 0, 0))
    return pl.pallas_call(
        functools.partial(_lru_body, T=T, ct=ct),
        grid=(B, C // ct, ns),
        in_specs=[tile, tile, pl.BlockSpec((CONV_WIDTH, ct), lambda b, c, s: (0, c)), vec,
                  wblk, vec, wblk, vec, vec],
        out_specs=tile,
        out_shape=jax.ShapeDtypeStruct((N, C), BF16),
        scratch_shapes=[pltpu.VMEM((T + 8, ct), F32), pltpu.VMEM((T, ct), F32),
                        pltpu.VMEM((T, ct), F32), pltpu.VMEM((8, ct), F32)],
        compiler_params=_params(("parallel", "parallel", "arbitrary")),
        name="rg_lru",
    )(u_lx, gelu_y, conv_w.astype(F32), row(conv_b), w_a.astype(BF16), row(b_a),
      w_x.astype(BF16), row(b_x), row(lam))


def _cmp_body(x_ref, pos_ref, w1_ref, w2_ref, o_ref, *, nc, half):
    x = x_ref[...]
    first = jnp.dot(x, w1_ref[0:half, :], preferred_element_type=F32)
    second = jnp.dot(x, w1_ref[half:, :], preferred_element_type=F32)
    pos = jnp.broadcast_to(pos_ref[...], (8, 2 * half)).astype(BF16)
    cpos = jnp.dot(pos, w1_ref[...], preferred_element_type=F32)[0:1, :]
    hid = first + pltpu.roll(second, nc - 1, axis=0) + cpos
    o_ref[...] = jnp.dot(_gelu(hid).astype(BF16), w2_ref[...],
                         preferred_element_type=F32).astype(o_ref.dtype)


def _compress(x, pos, w1, w2, *, B, name):
    G, rows, half = x.shape
    nc = rows // B
    hidden = w1.shape[1]
    return pl.pallas_call(
        functools.partial(_cmp_body, nc=nc, half=half),
        grid=(B, G),
        in_specs=[pl.BlockSpec((None, nc, half), lambda b, g: (g, b, 0)),
                  pl.BlockSpec((1, 2 * half), lambda b, g: (0, 0)),
                  pl.BlockSpec((2 * half, hidden), lambda b, g: (0, 0)),
                  pl.BlockSpec((hidden, HEAD_DIM), lambda b, g: (0, 0))],
        out_specs=pl.BlockSpec((None, None, nc, HEAD_DIM), lambda b, g: (b, g, 0, 0)),
        out_shape=jax.ShapeDtypeStruct((B, G, nc, HEAD_DIM), BF16),
        compiler_params=_params(("parallel", "parallel")),
        name=name,
    )(x, pos.reshape(1, 2 * half).astype(F32), w1.astype(BF16), w2.astype(BF16))


def _qk(q, k):
    return lax.dot_general(q, k, (((1,), (1,)), ((), ())), preferred_element_type=F32)


def _nsa_body(q_ref, kc_ref, vc_ref, ks_ref, vs_ref, kw_ref, vw_ref, g_ref, o_ref,
              kaug_ref, vaug_ref, vwaug_ref, vcaug_ref, ovt_ref, qaug_ref, m_ref, acc_ref,
              s_ref, part_ref, *, S, nc, n_sb, n_sel, tk):
    R = HPG * Q_BLOCK
    D2 = 2 * HEAD_DIM
    i = pl.program_id(2)
    s0 = i * Q_BLOCK

    @pl.when(i == 0)
    def _():
        key_blk = lax.broadcasted_iota(jnp.int32, (S, NB), 0) // SLC_BLOCK
        col = lax.broadcasted_iota(jnp.int32, (S, NB), 1)
        kaug_ref[:, 0:HEAD_DIM] = ks_ref[...]
        kaug_ref[:, HEAD_DIM:D2] = (key_blk == col).astype(BF16)
        vaug_ref[:, 0:HEAD_DIM] = vs_ref[...]
        vaug_ref[:, HEAD_DIM:D2] = jnp.ones((S, HEAD_DIM), BF16)
        vwaug_ref[:, 0:HEAD_DIM] = vw_ref[...]
        vwaug_ref[:, HEAD_DIM:D2] = jnp.ones((S, HEAD_DIM), BF16)
        vcaug_ref[:, 0:HEAD_DIM] = vc_ref[...]
        vcaug_ref[:, HEAD_DIM:D2] = jnp.ones((nc, HEAD_DIM), BF16)
        ci = lax.broadcasted_iota(jnp.int32, (NB, nc), 1) * CMP_STRIDE
        si = lax.broadcasted_iota(jnp.int32, (NB, nc), 0) * SLC_BLOCK
        ovt_ref[...] = ((ci < si + SLC_BLOCK) & (si < ci + CMP_BLOCK)).astype(BF16)

    qb = q_ref[...]
    q4 = jnp.concatenate([qb[:, h * HEAD_DIM:(h + 1) * HEAD_DIM] for h in range(HPG)], axis=0)
    tq = s0 + lax.broadcasted_iota(jnp.int32, (Q_BLOCK, 1), 0)

    def softmax_pv(s, bias, v_aug):
        k = s.shape[-1]
        s3 = s.reshape(HPG, Q_BLOCK, k) + bias[None]
        e = jnp.exp2(s3 - jnp.max(s3, axis=-1, keepdims=True))
        return e, jnp.dot(e.reshape(R, k).astype(BF16), v_aug, preferred_element_type=F32)

    cend = lax.broadcasted_iota(jnp.int32, (1, nc), 1) * CMP_STRIDE + (CMP_BLOCK - 1)
    e_c, oc = softmax_pv(_qk(q4, kc_ref[...]), jnp.where(cend <= tq, 0.0, NEG), vcaug_ref[...])
    inv_c = 1.0 / jnp.maximum(oc[:, HEAD_DIM:], 1e-30)
    any_c = (tq >= CMP_BLOCK - 1).astype(F32)
    o_cmp = (oc[:, :HEAD_DIM] * inv_c).reshape(HPG, Q_BLOCK, HEAD_DIM) * any_c[None]

    span = WINDOW + Q_BLOCK
    w0 = pl.multiple_of(jnp.maximum(s0 - WINDOW, 0), Q_BLOCK)
    diff = tq - (w0 + lax.broadcasted_iota(jnp.int32, (1, span), 1))
    _, ow = softmax_pv(_qk(q4, kw_ref[pl.ds(w0, span), :]),
                       jnp.where((diff >= 0) & (diff < WINDOW), 0.0, NEG),
                       vwaug_ref[pl.ds(w0, span), :])
    o_win = (ow[:, :HEAD_DIM] / jnp.maximum(ow[:, HEAD_DIM:], 1e-30)
             ).reshape(HPG, Q_BLOCK, HEAD_DIM)
    gt = g_ref[...]
    for h in range(HPG):
        part_ref[:, h * HEAD_DIM:(h + 1) * HEAD_DIM] = (
            gt[:, h:h + 1] * o_cmp[h] + gt[:, 2 * HPG + h:2 * HPG + h + 1] * o_win[h])

    inv_c3 = inv_c.reshape(HPG, Q_BLOCK, HEAD_DIM)
    pc_sum = e_c[0] * jnp.concatenate([inv_c3[0]] * (nc // HEAD_DIM), axis=1)
    for h in range(1, HPG):
        pc_sum = pc_sum + e_c[h] * jnp.concatenate([inv_c3[h]] * (nc // HEAD_DIM), axis=1)
    pc_hi = pc_sum.astype(BF16)
    pc_lo = (pc_sum - pc_hi.astype(F32)).astype(BF16)
    ovt = ovt_ref[...]
    imp = _qk(ovt, pc_hi) + _qk(ovt, pc_lo)
    blk = lax.broadcasted_iota(jnp.int32, (NB, 1), 0)
    cur = (s0 + lax.broadcasted_iota(jnp.int32, (1, Q_BLOCK), 1)) // SLC_BLOCK
    forced = (blk == 0) | (blk == cur) | (blk == cur - 1)
    score = jnp.where(forced, FORCED_SCORE, jnp.where(blk <= cur, imp, -FORCED_SCORE))
    if n_sb < NB:
        score = jnp.where(blk < n_sb, score, KNOCKED_OUT)
    blk_f = blk.astype(F32)
    sel = jnp.zeros((NB, Q_BLOCK), F32)
    for _ in range(n_sel):
        mx = jnp.max(score, axis=0, keepdims=True)
        first = jnp.min(jnp.where(score == mx, blk_f, float(NB)), axis=0, keepdims=True)
        hit = blk_f == first
        sel = jnp.where(hit, 1.0, sel)
        score = jnp.where(hit, KNOCKED_OUT, score)
    sel_past = jnp.where(blk < s0 // SLC_BLOCK, sel, 0.0)
    sel_bias = ((sel_past - 1.0) * (-NEG)).T.astype(BF16)
    qaug_ref[:, 0:HEAD_DIM] = q4
    for h in range(HPG):
        qaug_ref[h * Q_BLOCK:(h + 1) * Q_BLOCK, HEAD_DIM:D2] = sel_bias

    own = pl.ds(pl.multiple_of(s0, Q_BLOCK), Q_BLOCK)
    lower = (lax.broadcasted_iota(jnp.int32, (Q_BLOCK, Q_BLOCK), 1)
             <= lax.broadcasted_iota(jnp.int32, (Q_BLOCK, Q_BLOCK), 0))
    s_own = (_qk(q4, ks_ref[own, :]).reshape(HPG, Q_BLOCK, Q_BLOCK)
             + jnp.where(lower, 0.0, NEG)[None]).reshape(R, Q_BLOCK)
    m_ref[...] = s_own
    n_past = (s0 + tk - 1) // tk

    @pl.loop(0, n_past)
    def _(kt):
        k0 = pl.multiple_of(kt * tk, tk)
        s = _qk(qaug_ref[...], kaug_ref[pl.ds(k0, tk), :])
        s_ref[:, pl.ds(k0, tk)] = s
        m = m_ref[...]
        for c in range(tk // HEAD_DIM):
            m = jnp.maximum(m, s[:, c * HEAD_DIM:(c + 1) * HEAD_DIM])
        m_ref[...] = m

    m_row = jnp.max(m_ref[...], axis=-1, keepdims=True)
    acc_ref[...] = jnp.dot(jnp.exp2(s_own - m_row).astype(BF16), vaug_ref[own, :],
                           preferred_element_type=F32)
    m_ref[...] = jnp.broadcast_to(m_row, (R, HEAD_DIM))

    @pl.loop(0, n_past)
    def _(kt):
        k0 = pl.multiple_of(kt * tk, tk)
        m_b = jnp.concatenate([m_ref[...]] * (tk // HEAD_DIM), axis=1)
        p = jnp.exp2(s_ref[:, pl.ds(k0, tk)] - m_b).astype(BF16)
        acc_ref[...] += jnp.dot(p, vaug_ref[pl.ds(k0, tk), :], preferred_element_type=F32)

    acc_s = acc_ref[...]
    o_slc = (acc_s[:, :HEAD_DIM] / jnp.maximum(acc_s[:, HEAD_DIM:], 1e-30)
             ).reshape(HPG, Q_BLOCK, HEAD_DIM)

    gt = g_ref[...]
    for h in range(HPG):
        cols = slice(h * HEAD_DIM, (h + 1) * HEAD_DIM)
        o_ref[:, cols] = (part_ref[:, cols]
                          + gt[:, HPG + h:HPG + h + 1] * o_slc[h]).astype(o_ref.dtype)


def _nsa(q, kc, vc, kv6, gates, *, B, S, tk=1024):
    N = B * S
    nq = S // Q_BLOCK
    nc = kc.shape[2]
    n_sb = S // SLC_BLOCK
    assert n_sb <= NB and nc % HEAD_DIM == 0 and S % tk == 0
    G = N_KV_GROUPS
    gw = HPG * HEAD_DIM
    kv_spec = lambda t: pl.BlockSpec((None, None, S, HEAD_DIM), lambda b, g, i, t=t: (t, g, b, 0))
    cmp_spec = pl.BlockSpec((None, None, nc, HEAD_DIM), lambda b, g, i: (b, g, 0, 0))
    return pl.pallas_call(
        functools.partial(_nsa_body, S=S, nc=nc, n_sb=n_sb, n_sel=min(N_SELECT, n_sb), tk=tk),
        grid=(B, G, nq),
        in_specs=[pl.BlockSpec((Q_BLOCK, gw), lambda b, g, i: (b * nq + i, g)),
                  cmp_spec, cmp_spec, kv_spec(2), kv_spec(3), kv_spec(4), kv_spec(5),
                  pl.BlockSpec((Q_BLOCK, 128), lambda b, g, i: (b * nq + i, g))],
        out_specs=pl.BlockSpec((Q_BLOCK, gw), lambda b, g, i: (b * nq + i, g)),
        out_shape=jax.ShapeDtypeStruct((N, G * gw), BF16),
        scratch_shapes=[pltpu.VMEM((S, 2 * HEAD_DIM), BF16), pltpu.VMEM((S, 2 * HEAD_DIM), BF16),
                        pltpu.VMEM((S, 2 * HEAD_DIM), BF16), pltpu.VMEM((nc, 2 * HEAD_DIM), BF16),
                        pltpu.VMEM((NB, nc), BF16),
                        pltpu.VMEM((HPG * Q_BLOCK, 2 * HEAD_DIM), BF16),
                        pltpu.VMEM((HPG * Q_BLOCK, HEAD_DIM), F32),
                        pltpu.VMEM((HPG * Q_BLOCK, 2 * HEAD_DIM), F32),
                        pltpu.VMEM((HPG * Q_BLOCK, S), F32),
                        pltpu.VMEM((Q_BLOCK, HPG * HEAD_DIM), F32)],
        compiler_params=_params(("parallel", "parallel", "arbitrary")),
        name="nsa_attention",
    )(q, kc, vc, kv6, kv6, kv6, kv6, gates)


def _mlp_body(x_ref, g_ref, wu_ref, wd_ref, o_ref, h_ref, acc_ref):
    f = pl.program_id(1)

    @pl.when(f == 0)
    def _():
        h_ref[...] = _rms(x_ref[...], g_ref[...]).astype(BF16)
        acc_ref[...] = jnp.zeros_like(acc_ref)

    up = jnp.dot(h_ref[...], wu_ref[...], preferred_element_type=F32)
    up = jnp.square(jnp.maximum(up, 0.0)).astype(BF16)
    acc_ref[...] += jnp.dot(up, wd_ref[...], preferred_element_type=F32)

    @pl.when(f == pl.num_programs(1) - 1)
    def _():
        o_ref[...] = x_ref[...] + acc_ref[...]


def _mlp(x, g, w_up, w_down, *, tm=512, tf=512):
    M, D = x.shape
    FF = w_up.shape[1]
    return pl.pallas_call(
        _mlp_body,
        grid=(M // tm, FF // tf),
        in_specs=[pl.BlockSpec((tm, D), lambda i, f: (i, 0)),
                  pl.BlockSpec((1, D), lambda i, f: (0, 0)),
                  pl.BlockSpec((D, tf), lambda i, f: (0, f)),
                  pl.BlockSpec((tf, D), lambda i, f: (f, 0))],
        out_specs=pl.BlockSpec((tm, D), lambda i, f: (i, 0)),
        out_shape=jax.ShapeDtypeStruct((M, D), F32),
        scratch_shapes=[pltpu.VMEM((tm, D), BF16), pltpu.VMEM((tm, D), F32)],
        compiler_params=_params(("parallel", "arbitrary")),
        name="mlp",
    )(x, g.reshape(1, D).astype(F32), w_up, w_down)


def _ple_body(x_ref, p_ref, g_ref, wg_ref, wp_ref, gf_ref, o_ref, *, final_norm):
    x = x_ref[...]
    h = _rms(x, g_ref[...]).astype(BF16)
    gate = jax.nn.sigmoid(jnp.dot(h, wg_ref[...], preferred_element_type=F32))
    proj = jnp.dot(p_ref[...].astype(BF16), wp_ref[...], preferred_element_type=F32)
    y = x + gate * proj
    o_ref[...] = _rms(y, gf_ref[...]) if final_norm else y


def _ple(x, p, g, w_gate, w_proj, g_final, *, final_norm, tm=512):
    M, D = x.shape
    P = p.shape[1]
    row = lambda v: v.reshape(1, D).astype(F32)
    vec = pl.BlockSpec((1, D), lambda i: (0, 0))
    return pl.pallas_call(
        functools.partial(_ple_body, final_norm=final_norm),
        grid=(M // tm,),
        in_specs=[pl.BlockSpec((tm, D), lambda i: (i, 0)), pl.BlockSpec((tm, P), lambda i: (i, 0)),
                  vec, pl.BlockSpec((D, D), lambda i: (0, 0)), pl.BlockSpec((P, D), lambda i: (0, 0)),
                  vec],
        out_specs=pl.BlockSpec((tm, D), lambda i: (i, 0)),
        out_shape=jax.ShapeDtypeStruct((M, D), F32),
        compiler_params=_params(("parallel",)),
        name="ple_final",
    )(x, p, row(g), w_gate, w_proj, row(g_final))


def _layer(x, p, norm_mix_g, w_in, b_in, conv_w, conv_b, w_gate_a, b_gate_a, w_gate_x, b_gate_x,
           lru_lambda, w_lru_out, cmp_pos_k, cmp_w1_k, cmp_w2_k, cmp_pos_v, cmp_w1_v, cmp_w2_v,
           w_attn_out, w_o, norm_mlp_g, w_up, w_down, norm_ple_g, w_ple_gate, w_ple_proj,
           norm_final_g, *, B, S, final_norm):
    N, D = x.shape
    C = conv_w.shape[1]
    G = N_KV_GROUPS
    QW = N_HEADS * HEAD_DIM
    KVW = G * HEAD_DIM
    o_ly, o_q, o_kv, o_g, o_m = C, 2 * C, 2 * C + QW, 2 * C + QW + 6 * KVW, 2 * C + QW + 6 * KVW + 3 * N_HEADS
    wb = lambda lo, hi: (w_in[:, lo:hi].astype(BF16), b_in[lo:hi].reshape(1, hi - lo).astype(F32))
    proj = functools.partial(_mm, x, norm_g=norm_mix_g)

    w, b = wb(0, o_ly)
    u_lx = proj(w, lambda acc, b: acc + b, [(b, 'row', 0)], out_dtype=F32, name="in_lru_x")
    w, b = wb(o_ly, o_q)
    gelu_y = proj(w, lambda acc, b: _gelu(acc + b), [(b, 'row', 0)], out_dtype=BF16, name="in_lru_y")
    w, b = wb(o_q, o_kv)
    q = proj(w, lambda acc, b: (acc + b) * (HEAD_DIM ** -0.5 * LOG2E), [(b, 'row', 0)],
             out_dtype=BF16, name="in_q")
    w, b = wb(o_kv, o_g)
    kv6 = proj(w, lambda acc, b: acc + b, [(b, 'row', 0)], out_dtype=BF16, tn=KVW, split=G,
               name="in_kv")
    wg = w_in[:, o_g:o_m].reshape(D, 3, G, HPG).transpose(0, 2, 1, 3).reshape(D, G, 3 * HPG)
    wg = jnp.pad(wg, ((0, 0), (0, 0), (0, 128 - 3 * HPG))).reshape(D, G * 128).astype(BF16)
    bg = b_in[o_g:o_m].reshape(3, G, HPG).transpose(1, 0, 2).reshape(G, 3 * HPG)
    bg = jnp.pad(bg, ((0, 0), (0, 128 - 3 * HPG))).reshape(1, G * 128).astype(F32)
    gates = proj(wg, lambda acc, b: jax.nn.sigmoid(acc + b), [(bg, 'row', 0)], out_dtype=F32,
                 name="in_gates")
    w, b = wb(o_m, o_m + 2 * D)
    gm = proj(w, lambda acc, b: jax.nn.sigmoid(acc + b), [(b, 'row', 0)], out_dtype=BF16,
              name="in_merge_gates")

    a_in = _rg_lru(u_lx, gelu_y, conv_w, conv_b, w_gate_a, b_gate_a, w_gate_x, b_gate_x,
                   lru_lambda, B=B, S=S)
    nc = S // CMP_STRIDE
    kc = _compress(kv6[0].reshape(G, B * nc, CMP_STRIDE * HEAD_DIM), cmp_pos_k, cmp_w1_k, cmp_w2_k,
                   B=B, name="compress_k")
    vc = _compress(kv6[1].reshape(G, B * nc, CMP_STRIDE * HEAD_DIM), cmp_pos_v, cmp_w1_v, cmp_w2_v,
                   B=B, name="compress_v")
    o_nsa = _nsa(q, kc, vc, kv6, gates, B=B, S=S)

    nt = D // 512
    ya = _mm(a_in, w_lru_out.astype(BF16), lambda acc, ga: acc * ga.astype(F32),
             [(gm, 'tile', 0)], out_dtype=BF16, name="lru_out")
    mixed = _mm(o_nsa, w_attn_out.astype(BF16),
                lambda acc, ya, gb: ya.astype(F32) + acc * gb.astype(F32),
                [(ya, 'tile', 0), (gm, 'tile', nt)], out_dtype=BF16, name="attn_out_merge")
    x1 = _mm(mixed, w_o.astype(BF16), lambda acc, xr: xr + acc, [(x, 'tile', 0)],
             out_dtype=F32, name="out_proj")
    x2 = _mlp(x1, norm_mlp_g, w_up.astype(BF16), w_down.astype(BF16))
    return _ple(x2, p, norm_ple_g, w_ple_gate.astype(BF16), w_ple_proj.astype(BF16),
                norm_final_g, final_norm=final_norm)


def kernel(x, p, norm_mix_g, w_in, b_in, conv_w, conv_b, w_gate_a, b_gate_a, w_gate_x, b_gate_x, lru_lambda, w_lru_out, cmp_pos_k, cmp_w1_k, cmp_w2_k, cmp_pos_v, cmp_w1_v, cmp_w2_v, w_attn_out, w_o, norm_mlp_g, w_up, w_down, norm_ple_g, w_ple_gate, w_ple_proj, norm_final_g):
    B, S, D = x.shape
    depth = w_in.shape[0]
    assert S % 512 == 0 and S >= WINDOW + Q_BLOCK and D % 512 == 0
    xs = x.reshape(B * S, D)
    for i in range(depth):
        per_layer = [t[i] for t in (
            norm_mix_g, w_in, b_in, conv_w, conv_b, w_gate_a, b_gate_a, w_gate_x, b_gate_x,
            lru_lambda, w_lru_out, cmp_pos_k, cmp_w1_k, cmp_w2_k, cmp_pos_v, cmp_w1_v, cmp_w2_v,
            w_attn_out, w_o, norm_mlp_g, w_up, w_down, norm_ple_g, w_ple_gate, w_ple_proj)]
        xs = _layer(xs, p[i].reshape(B * S, -1), *per_layer, norm_final_g, B=B, S=S,
                    final_norm=i == depth - 1)
    return xs.reshape(B, S, D)
```

```python
import functools

import jax
import jax.numpy as jnp
from jax import lax
from jax.experimental import pallas as pl
from jax.experimental.pallas import tpu as pltpu

F32 = jnp.float32
BF16 = jnp.bfloat16

LRU_BLOCK_W = 128
CONV_WIDTH = 4
LRU_C = 8.0
N_HEADS = 16
N_KV_GROUPS = 4
HPG = N_HEADS // N_KV_GROUPS
HEAD_DIM = 128
CMP_STRIDE = 16
CMP_BLOCK = 32
SLC_BLOCK = 64
N_SELECT = 16
WINDOW = 512
Q_BLOCK = 128
EPS = 1e-6
NEG = -1e30
FORCED_SCORE = 1e4
KNOCKED_OUT = -3e38
NB = 128
LOG2E = 1.4426950408889634

VMEM_LIMIT = 56 * 1024 * 1024


def _params(sem):
    return pltpu.CompilerParams(dimension_semantics=sem, vmem_limit_bytes=VMEM_LIMIT)


def _rms(xf, g):
    return xf * lax.rsqrt(jnp.mean(xf * xf, axis=-1, keepdims=True) + EPS) * g


def _gelu(x):
    return jax.nn.gelu(x)


def _norm_body(x_ref, g_ref, o_ref):
    o_ref[...] = _rms(x_ref[...], g_ref[...]).astype(o_ref.dtype)


def _norm_cast(x, g, *, tm=1024):
    M, D = x.shape
    return pl.pallas_call(
        _norm_body,
        grid=(M // tm,),
        in_specs=[pl.BlockSpec((tm, D), lambda i: (i, 0)), pl.BlockSpec((1, D), lambda i: (0, 0))],
        out_specs=pl.BlockSpec((tm, D), lambda i: (i, 0)),
        out_shape=jax.ShapeDtypeStruct((M, D), BF16),
        compiler_params=_params(("parallel",)),
        name="norm_mix",
    )(x, g.reshape(1, D).astype(F32))


def _mm_body(*refs, n_extra, epi, split):
    a_ref, w_ref = refs[0], refs[1]
    extras = refs[2:2 + n_extra]
    o_ref = refs[2 + n_extra]
    acc = jnp.dot(a_ref[...], w_ref[...], preferred_element_type=F32)
    res = epi(acc, *[e[...] for e in extras])
    if split:
        for g in range(split):
            o_ref[g] = res[:, g * HEAD_DIM:(g + 1) * HEAD_DIM].astype(o_ref.dtype)
    else:
        o_ref[...] = res.astype(o_ref.dtype)


def _mm(a, w, epi, extras=(), *, out_dtype, tm=1024, tn=512, split=0, name):
    M, K = a.shape
    N = w.shape[1]
    tn = min(tn, N)
    grid = (M // tm, N // tn)
    in_specs = [pl.BlockSpec((tm, K), lambda i, j: (i, 0)),
                pl.BlockSpec((K, tn), lambda i, j: (0, j))]
    args = [a, w]
    for arr, kind, off in extras:
        if kind == 'row':
            in_specs.append(pl.BlockSpec((1, tn), lambda i, j, off=off: (0, j + off)))
        else:
            in_specs.append(pl.BlockSpec((tm, tn), lambda i, j, off=off: (i, j + off)))
        args.append(arr)
    if split:
        out_shape = jax.ShapeDtypeStruct((N // tn, split, M, HEAD_DIM), out_dtype)
        out_spec = pl.BlockSpec((None, split, tm, HEAD_DIM), lambda i, j: (j, 0, i, 0))
    else:
        out_shape = jax.ShapeDtypeStruct((M, N), out_dtype)
        out_spec = pl.BlockSpec((tm, tn), lambda i, j: (i, j))
    return pl.pallas_call(
        functools.partial(_mm_body, n_extra=len(extras), epi=epi, split=split),
        grid=grid, in_specs=in_specs, out_specs=out_spec, out_shape=out_shape,
        compiler_params=_params(("parallel", "arbitrary")),
        name=name,
    )(*args)


def _lru_body(u_ref, gy_ref, cw_ref, cb_ref, wa_ref, ba_ref, wx_ref, bx_ref, lam_ref,
              o_ref, xe_ref, a_ref, b_ref, h_ref, *, T, ct):
    s = pl.program_id(2)

    @pl.when(s == 0)
    def _():
        xe_ref[0:8, :] = jnp.zeros((8, ct), F32)
        h_ref[...] = jnp.zeros((8, ct), F32)

    xe_ref[8:, :] = u_ref[...]
    xc = cb_ref[...] + cw_ref[0:1, :] * xe_ref[pl.ds(8 - CONV_WIDTH + 1, T), :]
    for k in range(1, CONV_WIDTH):
        xc = xc + cw_ref[k:k + 1, :] * xe_ref[pl.ds(8 - CONV_WIDTH + 1 + k, T), :]
    xe_ref[0:8, :] = xe_ref[T:T + 8, :]

    lam = lam_ref[...]
    neg_softplus = -(jnp.maximum(-lam, 0.0) + jnp.log1p(jnp.exp(-jnp.abs(lam))))
    log_a_scale = LRU_C * neg_softplus
    row_in_group = lax.broadcasted_iota(jnp.int32, (T, LRU_BLOCK_W), 0) % 8
    xcb = xc.astype(BF16)
    for n in range(ct // LRU_BLOCK_W):
        sl = slice(n * LRU_BLOCK_W, (n + 1) * LRU_BLOCK_W)
        xb = xcb[:, sl]
        r = jax.nn.sigmoid(jnp.dot(xb, wa_ref[n], preferred_element_type=F32) + ba_ref[:, sl])
        i = jax.nn.sigmoid(jnp.dot(xb, wx_ref[n], preferred_element_type=F32) + bx_ref[:, sl])
        log_a = r * log_a_scale[:, sl]
        a = jnp.exp(log_a)
        mult = jnp.sqrt(jnp.maximum(1.0 - a * a, 0.0))
        b = mult * (i * xc[:, sl])
        for d in (1, 2, 4):
            m = row_in_group >= d
            a_sh = pltpu.roll(a, d, axis=0)
            b_sh = pltpu.roll(b, d, axis=0)
            b = jnp.where(m, a * b_sh + b, b)
            a = jnp.where(m, a * a_sh, a)
        a_ref[:, sl] = a
        b_ref[:, sl] = b

    def group(gi, hprev):
        r0 = pl.multiple_of(gi * 8, 8)
        ht = a_ref[pl.ds(r0, 8), :] * hprev + b_ref[pl.ds(r0, 8), :]
        b_ref[pl.ds(r0, 8), :] = ht
        return jnp.broadcast_to(ht[7:8, :], (8, ct))

    h_ref[...] = lax.fori_loop(0, T // 8, group, h_ref[...], unroll=8)
    o_ref[...] = (b_ref[...] * gy_ref[...].astype(F32)).astype(o_ref.dtype)


def _rg_lru(u_lx, gelu_y, conv_w, conv_b, w_a, b_a, w_x, b_x, lam, *, B, S, T=512, ct=512):
    N, C = u_lx.shape
    nb = ct // LRU_BLOCK_W
    ns = S // T
    row = lambda v: v.reshape(1, C).astype(F32)
    tile = pl.BlockSpec((T, ct), lambda b, c, s: (b * ns + s, c))
    vec = pl.BlockSpec((1, ct), lambda b, c, s: (0, c))
    wblk = pl.BlockSpec((nb, LRU_BLOCK_W, LRU_BLOCK_W), lambda b, c, s: (c, 0, 0))
    return pl.pallas_call(
        functools.partial(_lru_body, T=T, ct=ct),
        grid=(B, C // ct, ns),
        in_specs=[tile, tile, pl.BlockSpec((CONV_WIDTH, ct), lambda b, c, s: (0, c)), vec,
                  wblk, vec, wblk, vec, vec],
        out_specs=tile,
        out_shape=jax.ShapeDtypeStruct((N, C), BF16),
        scratch_shapes=[pltpu.VMEM((T + 8, ct), F32), pltpu.VMEM((T, ct), F32),
                        pltpu.VMEM((T, ct), F32), pltpu.VMEM((8, ct), F32)],
        compiler_params=_params(("parallel", "parallel", "arbitrary")),
        name="rg_lru",
    )(u_lx, gelu_y, conv_w.astype(F32), row(conv_b), w_a.astype(BF16), row(b_a),
      w_x.astype(BF16), row(b_x), row(lam))


def _cmp_body(x_ref, pos_ref, w1_ref, w2_ref, o_ref, *, nc, half):
    x = x_ref[...]
    first = jnp.dot(x, w1_ref[0:half, :], preferred_element_type=F32)
    second = jnp.dot(x, w1_ref[half:, :], preferred_element_type=F32)
    pos = jnp.broadcast_to(pos_ref[...], (8, 2 * half)).astype(BF16)
    cpos = jnp.dot(pos, w1_ref[...], preferred_element_type=F32)[0:1, :]
    hid = first + pltpu.roll(second, nc - 1, axis=0) + cpos
    o_ref[...] = jnp.dot(_gelu(hid).astype(BF16), w2_ref[...],
                         preferred_element_type=F32).astype(o_ref.dtype)


def _compress(x, pos, w1, w2, *, B, name):
    G, rows, half = x.shape
    nc = rows // B
    hidden = w1.shape[1]
    return pl.pallas_call(
        functools.partial(_cmp_body, nc=nc, half=half),
        grid=(B, G),
        in_specs=[pl.BlockSpec((None, nc, half), lambda b, g: (g, b, 0)),
                  pl.BlockSpec((1, 2 * half), lambda b, g: (0, 0)),
                  pl.BlockSpec((2 * half, hidden), lambda b, g: (0, 0)),
                  pl.BlockSpec((hidden, HEAD_DIM), lambda b, g: (0, 0))],
        out_specs=pl.BlockSpec((None, None, nc, HEAD_DIM), lambda b, g: (b, g, 0, 0)),
        out_shape=jax.ShapeDtypeStruct((B, G, nc, HEAD_DIM), BF16),
        compiler_params=_params(("parallel", "parallel")),
        name=name,
    )(x, pos.reshape(1, 2 * half).astype(F32), w1.astype(BF16), w2.astype(BF16))


def _qk(q, k):
    return lax.dot_general(q, k, (((1,), (1,)), ((), ())), preferred_element_type=F32)


def _nsa_body(q_ref, kc_ref, vc_ref, ks_ref, vs_ref, kw_ref, vw_ref, g_ref, o_ref,
              kaug_ref, vaug_ref, vwaug_ref, vcaug_ref, ovt_ref, qaug_ref, m_ref, acc_ref,
              s_ref, part_ref, *, S, nc, n_sb, n_sel, tk):
    R = HPG * Q_BLOCK
    D2 = 2 * HEAD_DIM
    i = pl.program_id(2)
    s0 = i * Q_BLOCK

    @pl.when(i == 0)
    def _():
        key_blk = lax.broadcasted_iota(jnp.int32, (S, NB), 0) // SLC_BLOCK
        col = lax.broadcasted_iota(jnp.int32, (S, NB), 1)
        kaug_ref[:, 0:HEAD_DIM] = ks_ref[...]
        kaug_ref[:, HEAD_DIM:D2] = (key_blk == col).astype(BF16)
        vaug_ref[:, 0:HEAD_DIM] = vs_ref[...]
        vaug_ref[:, HEAD_DIM:D2] = jnp.ones((S, HEAD_DIM), BF16)
        vwaug_ref[:, 0:HEAD_DIM] = vw_ref[...]
        vwaug_ref[:, HEAD_DIM:D2] = jnp.ones((S, HEAD_DIM), BF16)
        vcaug_ref[:, 0:HEAD_DIM] = vc_ref[...]
        vcaug_ref[:, HEAD_DIM:D2] = jnp.ones((nc, HEAD_DIM), BF16)
        ci = lax.broadcasted_iota(jnp.int32, (NB, nc), 1) * CMP_STRIDE
        si = lax.broadcasted_iota(jnp.int32, (NB, nc), 0) * SLC_BLOCK
        ovt_ref[...] = ((ci < si + SLC_BLOCK) & (si < ci + CMP_BLOCK)).astype(BF16)

    qb = q_ref[...]
    q4 = jnp.concatenate([qb[:, h * HEAD_DIM:(h + 1) * HEAD_DIM] for h in range(HPG)], axis=0)
    tq = s0 + lax.broadcasted_iota(jnp.int32, (Q_BLOCK, 1), 0)

    def softmax_pv(s, bias, v_aug):
        k = s.shape[-1]
        s3 = s.reshape(HPG, Q_BLOCK, k) + bias[None]
        e = jnp.exp2(s3 - jnp.max(s3, axis=-1, keepdims=True))
        return e, jnp.dot(e.reshape(R, k).astype(BF16), v_aug, preferred_element_type=F32)

    cend = lax.broadcasted_iota(jnp.int32, (1, nc), 1) * CMP_STRIDE + (CMP_BLOCK - 1)
    e_c, oc = softmax_pv(_qk(q4, kc_ref[...]), jnp.where(cend <= tq, 0.0, NEG), vcaug_ref[...])
    inv_c = 1.0 / jnp.maximum(oc[:, HEAD_DIM:], 1e-30)
    any_c = (tq >= CMP_BLOCK - 1).astype(F32)
    o_cmp = (oc[:, :HEAD_DIM] * inv_c).reshape(HPG, Q_BLOCK, HEAD_DIM) * any_c[None]

    span = WINDOW + Q_BLOCK
    w0 = pl.multiple_of(jnp.maximum(s0 - WINDOW, 0), Q_BLOCK)
    diff = tq - (w0 + lax.broadcasted_iota(jnp.int32, (1, span), 1))
    _, ow = softmax_pv(_qk(q4, kw_ref[pl.ds(w0, span), :]),
                       jnp.where((diff >= 0) & (diff < WINDOW), 0.0, NEG),
                       vwaug_ref[pl.ds(w0, span), :])
    o_win = (ow[:, :HEAD_DIM] / jnp.maximum(ow[:, HEAD_DIM:], 1e-30)
             ).reshape(HPG, Q_BLOCK, HEAD_DIM)
    gt = g_ref[...]
    for h in range(HPG):
        part_ref[:, h * HEAD_DIM:(h + 1) * HEAD_DIM] = (
            gt[:, h:h + 1] * o_cmp[h] + gt[:, 2 * HPG + h:2 * HPG + h + 1] * o_win[h])

    inv_c3 = inv_c.reshape(HPG, Q_BLOCK, HEAD_DIM)
    pc_sum = e_c[0] * jnp.concatenate([inv_c3[0]] * (nc // HEAD_DIM), axis=1)
    for h in range(1, HPG):
        pc_sum = pc_sum + e_c[h] * jnp.concatenate([inv_c3[h]] * (nc // HEAD_DIM), axis=1)
    pc_hi = pc_sum.astype(BF16)
    pc_lo = (pc_sum - pc_hi.astype(F32)).astype(BF16)
    ovt = ovt_ref[...]
    imp = _qk(ovt, pc_hi) + _qk(ovt, pc_lo)
    blk = lax.broadcasted_iota(jnp.int32, (NB, 1), 0)
    cur = (s0 + lax.broadcasted_iota(jnp.int32, (1, Q_BLOCK), 1)) // SLC_BLOCK
    forced = (blk == 0) | (blk == cur) | (blk == cur - 1)
    score = jnp.where(forced, KNOCKED_OUT, jnp.where(blk <= cur, imp, -FORCED_SCORE))
    if n_sb < NB:
        score = jnp.where(blk < n_sb, score, KNOCKED_OUT)
    blk_f = blk.astype(F32)
    sel = forced.astype(F32)
    for _ in range(n_sel - 3):
        mx = jnp.max(score, axis=0, keepdims=True)
        first = jnp.min(jnp.where(score == mx, blk_f, float(NB)), axis=0, keepdims=True)
        hit = blk_f == first
        sel = jnp.where(hit, 1.0, sel)
        score = jnp.where(hit, KNOCKED_OUT, score)
    sel_past = jnp.where(blk < s0 // SLC_BLOCK, sel, 0.0)
    sel_bias = ((sel_past - 1.0) * (-NEG)).T.astype(BF16)
    qaug_ref[:, 0:HEAD_DIM] = q4
    for h in range(HPG):
        qaug_ref[h * Q_BLOCK:(h + 1) * Q_BLOCK, HEAD_DIM:D2] = sel_bias

    own = pl.ds(pl.multiple_of(s0, Q_BLOCK), Q_BLOCK)
    lower = (lax.broadcasted_iota(jnp.int32, (Q_BLOCK, Q_BLOCK), 1)
             <= lax.broadcasted_iota(jnp.int32, (Q_BLOCK, Q_BLOCK), 0))
    s_own = (_qk(q4, ks_ref[own, :]).reshape(HPG, Q_BLOCK, Q_BLOCK)
             + jnp.where(lower, 0.0, NEG)[None]).reshape(R, Q_BLOCK)
    m_ref[...] = s_own
    n_past = (s0 + tk - 1) // tk

    @pl.loop(0, n_past)
    def _(kt):
        k0 = pl.multiple_of(kt * tk, tk)
        s = _qk(qaug_ref[...], kaug_ref[pl.ds(k0, tk), :])
        s_ref[:, pl.ds(k0, tk)] = s
        m = m_ref[...]
        for c in range(tk // HEAD_DIM):
            m = jnp.maximum(m, s[:, c * HEAD_DIM:(c + 1) * HEAD_DIM])
        m_ref[...] = m

    m_row = jnp.max(m_ref[...], axis=-1, keepdims=True)
    acc_ref[...] = jnp.dot(jnp.exp2(s_own - m_row).astype(BF16), vaug_ref[own, :],
                           preferred_element_type=F32)
    m_ref[...] = jnp.broadcast_to(m_row, (R, HEAD_DIM))

    @pl.loop(0, n_past)
    def _(kt):
        k0 = pl.multiple_of(kt * tk, tk)
        m_b = jnp.concatenate([m_ref[...]] * (tk // HEAD_DIM), axis=1)
        p = jnp.exp2(s_ref[:, pl.ds(k0, tk)] - m_b).astype(BF16)
        acc_ref[...] += jnp.dot(p, vaug_ref[pl.ds(k0, tk), :], preferred_element_type=F32)

    acc_s = acc_ref[...]
    o_slc = (acc_s[:, :HEAD_DIM] / jnp.maximum(acc_s[:, HEAD_DIM:], 1e-30)
             ).reshape(HPG, Q_BLOCK, HEAD_DIM)

    gt = g_ref[...]
    for h in range(HPG):
        cols = slice(h * HEAD_DIM, (h + 1) * HEAD_DIM)
        o_ref[:, cols] = (part_ref[:, cols]
                          + gt[:, HPG + h:HPG + h + 1] * o_slc[h]).astype(o_ref.dtype)


def _nsa(q, kc, vc, kv6, gates, *, B, S, tk=2048):
    N = B * S
    nq = S // Q_BLOCK
    nc = kc.shape[2]
    n_sb = S // SLC_BLOCK
    assert n_sb <= NB and nc % HEAD_DIM == 0 and S % tk == 0
    G = N_KV_GROUPS
    gw = HPG * HEAD_DIM
    kv_spec = lambda t: pl.BlockSpec((None, None, S, HEAD_DIM), lambda b, g, i, t=t: (t, g, b, 0))
    cmp_spec = pl.BlockSpec((None, None, nc, HEAD_DIM), lambda b, g, i: (b, g, 0, 0))
    return pl.pallas_call(
        functools.partial(_nsa_body, S=S, nc=nc, n_sb=n_sb, n_sel=min(N_SELECT, n_sb), tk=tk),
        grid=(B, G, nq),
        in_specs=[pl.BlockSpec((Q_BLOCK, gw), lambda b, g, i: (b * nq + i, g)),
                  cmp_spec, cmp_spec, kv_spec(2), kv_spec(3), kv_spec(4), kv_spec(5),
                  pl.BlockSpec((Q_BLOCK, 128), lambda b, g, i: (b * nq + i, g))],
        out_specs=pl.BlockSpec((Q_BLOCK, gw), lambda b, g, i: (b * nq + i, g)),
        out_shape=jax.ShapeDtypeStruct((N, G * gw), BF16),
        scratch_shapes=[pltpu.VMEM((S, 2 * HEAD_DIM), BF16), pltpu.VMEM((S, 2 * HEAD_DIM), BF16),
                        pltpu.VMEM((S, 2 * HEAD_DIM), BF16), pltpu.VMEM((nc, 2 * HEAD_DIM), BF16),
                        pltpu.VMEM((NB, nc), BF16),
                        pltpu.VMEM((HPG * Q_BLOCK, 2 * HEAD_DIM), BF16),
                        pltpu.VMEM((HPG * Q_BLOCK, HEAD_DIM), F32),
                        pltpu.VMEM((HPG * Q_BLOCK, 2 * HEAD_DIM), F32),
                        pltpu.VMEM((HPG * Q_BLOCK, S), F32),
                        pltpu.VMEM((Q_BLOCK, HPG * HEAD_DIM), F32)],
        compiler_params=_params(("parallel", "parallel", "arbitrary")),
        name="nsa_attention",
    )(q, kc, vc, kv6, kv6, kv6, kv6, gates)


def _mlp_body(x_ref, g_ref, wu_ref, wd_ref, o_ref, h_ref, acc_ref):
    f = pl.program_id(1)

    @pl.when(f == 0)
    def _():
        h_ref[...] = _rms(x_ref[...], g_ref[...]).astype(BF16)
        acc_ref[...] = jnp.zeros_like(acc_ref)

    up = jnp.dot(h_ref[...], wu_ref[...], preferred_element_type=F32)
    up = jnp.square(jnp.maximum(up, 0.0)).astype(BF16)
    acc_ref[...] += jnp.dot(up, wd_ref[...], preferred_element_type=F32)

    @pl.when(f == pl.num_programs(1) - 1)
    def _():
        o_ref[...] = x_ref[...] + acc_ref[...]


def _mlp(x, g, w_up, w_down, *, tm=512, tf=512):
    M, D = x.shape
    FF = w_up.shape[1]
    return pl.pallas_call(
        _mlp_body,
        grid=(M // tm, FF // tf),
        in_specs=[pl.BlockSpec((tm, D), lambda i, f: (i, 0)),
                  pl.BlockSpec((1, D), lambda i, f: (0, 0)),
                  pl.BlockSpec((D, tf), lambda i, f: (0, f)),
                  pl.BlockSpec((tf, D), lambda i, f: (f, 0))],
        out_specs=pl.BlockSpec((tm, D), lambda i, f: (i, 0)),
        out_shape=jax.ShapeDtypeStruct((M, D), F32),
        scratch_shapes=[pltpu.VMEM((tm, D), BF16), pltpu.VMEM((tm, D), F32)],
        compiler_params=_params(("parallel", "arbitrary")),
        name="mlp",
    )(x, g.reshape(1, D).astype(F32), w_up, w_down)


def _ple_body(x_ref, p_ref, g_ref, wg_ref, wp_ref, gf_ref, o_ref, *, final_norm):
    x = x_ref[...]
    h = _rms(x, g_ref[...]).astype(BF16)
    gate = jax.nn.sigmoid(jnp.dot(h, wg_ref[...], preferred_element_type=F32))
    proj = jnp.dot(p_ref[...].astype(BF16), wp_ref[...], preferred_element_type=F32)
    y = x + gate * proj
    o_ref[...] = _rms(y, gf_ref[...]) if final_norm else y


def _ple(x, p, g, w_gate, w_proj, g_final, *, final_norm, tm=512):
    M, D = x.shape
    P = p.shape[1]
    row = lambda v: v.reshape(1, D).astype(F32)
    vec = pl.BlockSpec((1, D), lambda i: (0, 0))
    return pl.pallas_call(
        functools.partial(_ple_body, final_norm=final_norm),
        grid=(M // tm,),
        in_specs=[pl.BlockSpec((tm, D), lambda i: (i, 0)), pl.BlockSpec((tm, P), lambda i: (i, 0)),
                  vec, pl.BlockSpec((D, D), lambda i: (0, 0)), pl.BlockSpec((P, D), lambda i: (0, 0)),
                  vec],
        out_specs=pl.BlockSpec((tm, D), lambda i: (i, 0)),
        out_shape=jax.ShapeDtypeStruct((M, D), F32),
        compiler_params=_params(("parallel",)),
        name="ple_final",
    )(x, p, row(g), w_gate, w_proj, row(g_final))


def _layer(x, p, norm_mix_g, w_in, b_in, conv_w, conv_b, w_gate_a, b_gate_a, w_gate_x, b_gate_x,
           lru_lambda, w_lru_out, cmp_pos_k, cmp_w1_k, cmp_w2_k, cmp_pos_v, cmp_w1_v, cmp_w2_v,
           w_attn_out, w_o, norm_mlp_g, w_up, w_down, norm_ple_g, w_ple_gate, w_ple_proj,
           norm_final_g, *, B, S, final_norm):
    N, D = x.shape
    C = conv_w.shape[1]
    G = N_KV_GROUPS
    QW = N_HEADS * HEAD_DIM
    KVW = G * HEAD_DIM
    o_ly, o_q, o_kv, o_g, o_m = C, 2 * C, 2 * C + QW, 2 * C + QW + 6 * KVW, 2 * C + QW + 6 * KVW + 3 * N_HEADS
    wb = lambda lo, hi: (w_in[:, lo:hi].astype(BF16), b_in[lo:hi].reshape(1, hi - lo).astype(F32))
    proj = functools.partial(_mm, _norm_cast(x, norm_mix_g))

    w, b = wb(0, o_ly)
    u_lx = proj(w, lambda acc, b: acc + b, [(b, 'row', 0)], out_dtype=F32, name="in_lru_x")
    w, b = wb(o_ly, o_q)
    gelu_y = proj(w, lambda acc, b: _gelu(acc + b), [(b, 'row', 0)], out_dtype=BF16, name="in_lru_y")
    w, b = wb(o_q, o_kv)
    q = proj(w, lambda acc, b: (acc + b) * (HEAD_DIM ** -0.5 * LOG2E), [(b, 'row', 0)],
             out_dtype=BF16, name="in_q")
    w, b = wb(o_kv, o_g)
    kv6 = proj(w, lambda acc, b: acc + b, [(b, 'row', 0)], out_dtype=BF16, tn=KVW, split=G,
               name="in_kv")
    wg = w_in[:, o_g:o_m].reshape(D, 3, G, HPG).transpose(0, 2, 1, 3).reshape(D, G, 3 * HPG)
    wg = jnp.pad(wg, ((0, 0), (0, 0), (0, 128 - 3 * HPG))).reshape(D, G * 128).astype(BF16)
    bg = b_in[o_g:o_m].reshape(3, G, HPG).transpose(1, 0, 2).reshape(G, 3 * HPG)
    bg = jnp.pad(bg, ((0, 0), (0, 128 - 3 * HPG))).reshape(1, G * 128).astype(F32)
    gates = proj(wg, lambda acc, b: jax.nn.sigmoid(acc + b), [(bg, 'row', 0)], out_dtype=F32,
                 name="in_gates")
    w, b = wb(o_m, o_m + 2 * D)
    gm = proj(w, lambda acc, b: jax.nn.sigmoid(acc + b), [(b, 'row', 0)], out_dtype=BF16,
              name="in_merge_gates")

    a_in = _rg_lru(u_lx, gelu_y, conv_w, conv_b, w_gate_a, b_gate_a, w_gate_x, b_gate_x,
                   lru_lambda, B=B, S=S)
    nc = S // CMP_STRIDE
    kc = _compress(kv6[0].reshape(G, B * nc, CMP_STRIDE * HEAD_DIM), cmp_pos_k, cmp_w1_k, cmp_w2_k,
                   B=B, name="compress_k")
    vc = _compress(kv6[1].reshape(G, B * nc, CMP_STRIDE * HEAD_DIM), cmp_pos_v, cmp_w1_v, cmp_w2_v,
                   B=B, name="compress_v")
    o_nsa = _nsa(q, kc, vc, kv6, gates, B=B, S=S)

    nt = D // 512
    ya = _mm(a_in, w_lru_out.astype(BF16), lambda acc, ga: acc * ga.astype(F32),
             [(gm, 'tile', 0)], out_dtype=BF16, name="lru_out")
    mixed = _mm(o_nsa, w_attn_out.astype(BF16),
                lambda acc, ya, gb: ya.astype(F32) + acc * gb.astype(F32),
                [(ya, 'tile', 0), (gm, 'tile', nt)], out_dtype=BF16, name="attn_out_merge")
    x1 = _mm(mixed, w_o.astype(BF16), lambda acc, xr: xr + acc, [(x, 'tile', 0)],
             out_dtype=F32, name="out_proj")
    x2 = _mlp(x1, norm_mlp_g, w_up.astype(BF16), w_down.astype(BF16))
    return _ple(x2, p, norm_ple_g, w_ple_gate.astype(BF16), w_ple_proj.astype(BF16),
                norm_final_g, final_norm=final_norm)


def kernel(x, p, norm_mix_g, w_in, b_in, conv_w, conv_b, w_gate_a, b_gate_a, w_gate_x, b_gate_x, lru_lambda, w_lru_out, cmp_pos_k, cmp_w1_k, cmp_w2_k, cmp_pos_v, cmp_w1_v, cmp_w2_v, w_attn_out, w_o, norm_mlp_g, w_up, w_down, norm_ple_g, w_ple_gate, w_ple_proj, norm_final_g):
    B, S, D = x.shape
    depth = w_in.shape[0]
    assert S % 512 == 0 and S >= WINDOW + Q_BLOCK and D % 512 == 0
    xs = x.reshape(B * S, D)
    for i in range(depth):
        per_layer = [t[i] for t in (
            norm_mix_g, w_in, b_in, conv_w, conv_b, w_gate_a, b_gate_a, w_gate_x, b_gate_x,
            lru_lambda, w_lru_out, cmp_pos_k, cmp_w1_k, cmp_w2_k, cmp_pos_v, cmp_w1_v, cmp_w2_v,
            w_attn_out, w_o, norm_mlp_g, w_up, w_down, norm_ple_g, w_ple_gate, w_ple_proj)]
        xs = _layer(xs, p[i].reshape(B * S, -1), *per_layer, norm_final_g, B=B, S=S,
                    final_norm=i == depth - 1)
    return xs.reshape(B, S, D)
```

```python
import functools

import jax
import jax.numpy as jnp
from jax import lax
from jax.experimental import pallas as pl
from jax.experimental.pallas import tpu as pltpu

F32 = jnp.float32
BF16 = jnp.bfloat16

LRU_BLOCK_W = 128
CONV_WIDTH = 4
LRU_C = 8.0
N_HEADS = 16
N_KV_GROUPS = 4
HPG = N_HEADS // N_KV_GROUPS
HEAD_DIM = 128
CMP_STRIDE = 16
CMP_BLOCK = 32
SLC_BLOCK = 64
N_SELECT = 16
WINDOW = 512
Q_BLOCK = 128
EPS = 1e-6
NEG = -1e30
FORCED_SCORE = 1e4
KNOCKED_OUT = -3e38
NB = 128
LOG2E = 1.4426950408889634

VMEM_LIMIT = 56 * 1024 * 1024


def _params(sem):
    return pltpu.CompilerParams(dimension_semantics=sem, vmem_limit_bytes=VMEM_LIMIT)


def _rms(xf, g):
    return xf * lax.rsqrt(jnp.mean(xf * xf, axis=-1, keepdims=True) + EPS) * g


def _gelu(x):
    return jax.nn.gelu(x)


def _norm_body(x_ref, g_ref, o_ref):
    o_ref[...] = _rms(x_ref[...], g_ref[...]).astype(o_ref.dtype)


def _norm_cast(x, g, *, tm=1024):
    M, D = x.shape
    return pl.pallas_call(
        _norm_body,
        grid=(M // tm,),
        in_specs=[pl.BlockSpec((tm, D), lambda i: (i, 0)), pl.BlockSpec((1, D), lambda i: (0, 0))],
        out_specs=pl.BlockSpec((tm, D), lambda i: (i, 0)),
        out_shape=jax.ShapeDtypeStruct((M, D), BF16),
        compiler_params=_params(("parallel",)),
        name="norm_mix",
    )(x, g.reshape(1, D).astype(F32))


def _mm_body(*refs, n_extra, epi, split):
    a_ref, w_ref = refs[0], refs[1]
    extras = refs[2:2 + n_extra]
    o_ref = refs[2 + n_extra]
    acc = jnp.dot(a_ref[...], w_ref[...], preferred_element_type=F32)
    res = epi(acc, *[e[...] for e in extras])
    if split:
        for g in range(split):
            o_ref[g] = res[:, g * HEAD_DIM:(g + 1) * HEAD_DIM].astype(o_ref.dtype)
    else:
        o_ref[...] = res.astype(o_ref.dtype)


def _mm(a, w, epi, extras=(), *, out_dtype, tm=1024, tn=512, split=0, name):
    M, K = a.shape
    N = w.shape[1]
    tn = min(tn, N)
    grid = (M // tm, N // tn)
    in_specs = [pl.BlockSpec((tm, K), lambda i, j: (i, 0)),
                pl.BlockSpec((K, tn), lambda i, j: (0, j))]
    args = [a, w]
    for arr, kind, off in extras:
        if kind == 'row':
            in_specs.append(pl.BlockSpec((1, tn), lambda i, j, off=off: (0, j + off)))
        else:
            in_specs.append(pl.BlockSpec((tm, tn), lambda i, j, off=off: (i, j + off)))
        args.append(arr)
    if split:
        out_shape = jax.ShapeDtypeStruct((N // tn, split, M, HEAD_DIM), out_dtype)
        out_spec = pl.BlockSpec((None, split, tm, HEAD_DIM), lambda i, j: (j, 0, i, 0))
    else:
        out_shape = jax.ShapeDtypeStruct((M, N), out_dtype)
        out_spec = pl.BlockSpec((tm, tn), lambda i, j: (i, j))
    return pl.pallas_call(
        functools.partial(_mm_body, n_extra=len(extras), epi=epi, split=split),
        grid=grid, in_specs=in_specs, out_specs=out_spec, out_shape=out_shape,
        compiler_params=_params(("parallel", "arbitrary")),
        name=name,
    )(*args)


def _lru_body(u_ref, gy_ref, cw_ref, cb_ref, wa_ref, ba_ref, wx_ref, bx_ref, lam_ref,
              o_ref, xe_ref, xc_ref, a_ref, b_ref, hn_ref, h_ref, *, T, ct):
    L = T // 8
    W = LRU_BLOCK_W
    nb = ct // W
    s = pl.program_id(2)

    @pl.when(s == 0)
    def _():
        xe_ref[:, 0:8, :] = jnp.zeros((nb, 8, W), F32)
        h_ref[...] = jnp.zeros((nb, 8, W), F32)

    lam = lam_ref[...]
    neg_softplus = -(jnp.maximum(-lam, 0.0) + jnp.log1p(jnp.exp(-jnp.abs(lam))))
    log_a_scale = LRU_C * neg_softplus
    for n in range(nb):
        sl = slice(n * W, (n + 1) * W)
        xe_ref[n, 8:, :] = u_ref[:, sl]

        def seg_rows(off):
            return xe_ref[n, pl.ds(off, 8, stride=L), :]

        taps = [jnp.broadcast_to(cw_ref[k:k + 1, sl], (8, W)) for k in range(CONV_WIDTH)]
        bias = jnp.broadcast_to(cb_ref[:, sl], (8, W))
        prev = [seg_rows(8 - CONV_WIDTH + 1 + k) for k in range(CONV_WIDTH - 1)]
        for t in range(L):
            win = prev + [seg_rows(8 + t)]
            y = bias
            for k in range(CONV_WIDTH):
                y = y + taps[k] * win[k]
            xc_ref[t * 8:(t + 1) * 8, :] = y
            prev = win[1:]
        xe_ref[n, 0:8, :] = xe_ref[n, T:T + 8, :]

        xc = xc_ref[...]
        xb = xc.astype(BF16)
        r = jax.nn.sigmoid(jnp.dot(xb, wa_ref[n], preferred_element_type=F32) + ba_ref[:, sl])
        i = jax.nn.sigmoid(jnp.dot(xb, wx_ref[n], preferred_element_type=F32) + bx_ref[:, sl])
        a = jnp.exp(r * log_a_scale[:, sl])
        a_ref[n] = a
        b_ref[n] = jnp.sqrt(jnp.maximum(1.0 - a * a, 0.0)) * (i * xc)

    def step(t, carry):
        rows = pl.ds(pl.multiple_of(t * 8, 8), 8)
        out = []
        for n in range(nb):
            h, decay = carry[n]
            a_t = a_ref[n, rows, :]
            h = a_t * h + b_ref[n, rows, :]
            decay = a_t * decay
            b_ref[n, rows, :] = h
            a_ref[n, rows, :] = decay
            out.append((h, decay))
        return tuple(out)

    init = tuple((jnp.zeros((8, W), F32), jnp.ones((8, W), F32)) for _ in range(nb))
    ends = lax.fori_loop(0, L, step, init, unroll=8)
    row = lax.broadcasted_iota(jnp.int32, (8, W), 0)
    for n in range(nb):
        h_end, a_end = ends[n]
        h_in = h_ref[n]
        for r in range(1, 8):
            h_in = jnp.where(row == r, pltpu.roll(a_end * h_in + h_end, 1, axis=0), h_in)
        h_ref[n] = jnp.broadcast_to((a_end * h_in + h_end)[7:8, :], (8, W))
        for t in range(L):
            rows = slice(t * 8, (t + 1) * 8)
            hn_ref[pl.ds(t, 8, stride=L), :] = b_ref[n, rows, :] + a_ref[n, rows, :] * h_in
        sl = slice(n * W, (n + 1) * W)
        o_ref[:, sl] = (hn_ref[...] * gy_ref[:, sl].astype(F32)).astype(o_ref.dtype)


def _rg_lru(u_lx, gelu_y, conv_w, conv_b, w_a, b_a, w_x, b_x, lam, *, B, S, T=512, ct=512):
    N, C = u_lx.shape
    nb = ct // LRU_BLOCK_W
    ns = S // T
    row = lambda v: v.reshape(1, C).astype(F32)
    tile = pl.BlockSpec((T, ct), lambda b, c, s: (b * ns + s, c))
    vec = pl.BlockSpec((1, ct), lambda b, c, s: (0, c))
    wblk = pl.BlockSpec((nb, LRU_BLOCK_W, LRU_BLOCK_W), lambda b, c, s: (c, 0, 0))
    return pl.pallas_call(
        functools.partial(_lru_body, T=T, ct=ct),
        grid=(B, C // ct, ns),
        in_specs=[tile, tile, pl.BlockSpec((CONV_WIDTH, ct), lambda b, c, s: (0, c)), vec,
                  wblk, vec, wblk, vec, vec],
        out_specs=tile,
        out_shape=jax.ShapeDtypeStruct((N, C), BF16),
        scratch_shapes=[pltpu.VMEM((nb, T + 8, LRU_BLOCK_W), F32), pltpu.VMEM((T, LRU_BLOCK_W), F32),
                        pltpu.VMEM((nb, T, LRU_BLOCK_W), F32), pltpu.VMEM((nb, T, LRU_BLOCK_W), F32),
                        pltpu.VMEM((T, LRU_BLOCK_W), F32), pltpu.VMEM((nb, 8, LRU_BLOCK_W), F32)],
        compiler_params=_params(("parallel", "parallel", "arbitrary")),
        name="rg_lru",
    )(u_lx, gelu_y, conv_w.astype(F32), row(conv_b), w_a.astype(BF16), row(b_a),
      w_x.astype(BF16), row(b_x), row(lam))


def _cmp_body(x_ref, pos_ref, w1_ref, w2_ref, o_ref, *, nc, half):
    x = x_ref[...]
    first = jnp.dot(x, w1_ref[0:half, :], preferred_element_type=F32)
    second = jnp.dot(x, w1_ref[half:, :], preferred_element_type=F32)
    pos = jnp.broadcast_to(pos_ref[...], (8, 2 * half)).astype(BF16)
    cpos = jnp.dot(pos, w1_ref[...], preferred_element_type=F32)[0:1, :]
    hid = first + pltpu.roll(second, nc - 1, axis=0) + cpos
    o_ref[...] = jnp.dot(_gelu(hid).astype(BF16), w2_ref[...],
                         preferred_element_type=F32).astype(o_ref.dtype)


def _compress(x, t, pos, w1, w2, *, B, name):
    _, G, rows, half = x.shape
    nc = rows // B
    hidden = w1.shape[1]
    return pl.pallas_call(
        functools.partial(_cmp_body, nc=nc, half=half),
        grid=(B, G),
        in_specs=[pl.BlockSpec((None, None, nc, half), lambda b, g: (t, g, b, 0)),
                  pl.BlockSpec((1, 2 * half), lambda b, g: (0, 0)),
                  pl.BlockSpec((2 * half, hidden), lambda b, g: (0, 0)),
                  pl.BlockSpec((hidden, HEAD_DIM), lambda b, g: (0, 0))],
        out_specs=pl.BlockSpec((None, None, nc, HEAD_DIM), lambda b, g: (b, g, 0, 0)),
        out_shape=jax.ShapeDtypeStruct((B, G, nc, HEAD_DIM), BF16),
        compiler_params=_params(("parallel", "parallel")),
        name=name,
    )(x, pos.reshape(1, 2 * half).astype(F32), w1.astype(BF16), w2.astype(BF16))


def _qk(q, k):
    return lax.dot_general(q, k, (((1,), (1,)), ((), ())), preferred_element_type=F32)


def _nsa_body(q_ref, kc_ref, vc_ref, ks_ref, vs_ref, kw_ref, vw_ref, g_ref, o_ref,
              kaug_ref, vaug_ref, vwaug_ref, ovt_ref, qaug_ref, m_ref, acc_ref,
              s_ref, part_ref, own_ref, mown_ref, *, S, nc, n_sb, n_sel, tk):
    R = HPG * Q_BLOCK
    D2 = 2 * HEAD_DIM
    i = pl.program_id(2)
    s0 = i * Q_BLOCK

    @pl.when(i == 0)
    def _():
        key_blk = lax.broadcasted_iota(jnp.int32, (S, NB), 0) // SLC_BLOCK
        col = lax.broadcasted_iota(jnp.int32, (S, NB), 1)
        kaug_ref[:, 0:HEAD_DIM] = ks_ref[...]
        kaug_ref[:, HEAD_DIM:D2] = (key_blk == col).astype(BF16)
        vaug_ref[:, 0:HEAD_DIM] = vs_ref[...]
        vaug_ref[:, HEAD_DIM:D2] = jnp.ones((S, HEAD_DIM), BF16)
        vwaug_ref[:, 0:HEAD_DIM] = vw_ref[...]
        vwaug_ref[:, HEAD_DIM:D2] = jnp.ones((S, HEAD_DIM), BF16)
        ci = lax.broadcasted_iota(jnp.int32, (NB, nc), 1) * CMP_STRIDE
        si = lax.broadcasted_iota(jnp.int32, (NB, nc), 0) * SLC_BLOCK
        ovt_ref[...] = ((ci < si + SLC_BLOCK) & (si < ci + CMP_BLOCK)).astype(BF16)

    qb = q_ref[...]
    q4 = jnp.concatenate([qb[:, h * HEAD_DIM:(h + 1) * HEAD_DIM] for h in range(HPG)], axis=0)
    tq = s0 + lax.broadcasted_iota(jnp.int32, (Q_BLOCK, 1), 0)

    def softmax_pv(s, bias, v_aug):
        k = s.shape[-1]
        s3 = s.reshape(HPG, Q_BLOCK, k) + bias[None]
        e = jnp.exp2(s3 - jnp.max(s3, axis=-1, keepdims=True))
        return e, jnp.dot(e.reshape(R, k).astype(BF16), v_aug, preferred_element_type=F32)

    cend = lax.broadcasted_iota(jnp.int32, (1, nc), 1) * CMP_STRIDE + (CMP_BLOCK - 1)
    e_c, oc = softmax_pv(_qk(q4, kc_ref[...]), jnp.where(cend <= tq, 0.0, NEG), vc_ref[...])
    inv_c = 1.0 / jnp.maximum(jnp.sum(e_c, axis=-1, keepdims=True), 1e-30)
    any_c = (tq >= CMP_BLOCK - 1).astype(F32)
    o_cmp = oc.reshape(HPG, Q_BLOCK, HEAD_DIM) * (inv_c * any_c[None])

    span = WINDOW + Q_BLOCK
    w0 = pl.multiple_of(jnp.maximum(s0 - WINDOW, 0), Q_BLOCK)
    diff = tq - (w0 + lax.broadcasted_iota(jnp.int32, (1, span), 1))
    _, ow = softmax_pv(_qk(q4, kw_ref[pl.ds(w0, span), :]),
                       jnp.where((diff >= 0) & (diff < WINDOW), 0.0, NEG),
                       vwaug_ref[pl.ds(w0, span), :])
    o_win = (ow[:, :HEAD_DIM] / jnp.maximum(ow[:, HEAD_DIM:], 1e-30)
             ).reshape(HPG, Q_BLOCK, HEAD_DIM)
    gt = g_ref[...]
    for h in range(HPG):
        part_ref[:, h * HEAD_DIM:(h + 1) * HEAD_DIM] = (
            gt[:, h:h + 1] * o_cmp[h] + gt[:, 2 * HPG + h:2 * HPG + h + 1] * o_win[h])

    pc_sum = e_c[0] * inv_c[0]
    for h in range(1, HPG):
        pc_sum = pc_sum + e_c[h] * inv_c[h]
    pc_hi = pc_sum.astype(BF16)
    pc_lo = (pc_sum - pc_hi.astype(F32)).astype(BF16)
    ovt = ovt_ref[...]
    imp = _qk(ovt, pc_hi) + _qk(ovt, pc_lo)
    blk = lax.broadcasted_iota(jnp.int32, (NB, 1), 0)
    cur = (s0 + lax.broadcasted_iota(jnp.int32, (1, Q_BLOCK), 1)) // SLC_BLOCK
    forced = (blk == 0) | (blk == cur) | (blk == cur - 1)
    score = jnp.where(forced, KNOCKED_OUT, jnp.where(blk <= cur, imp, -FORCED_SCORE))
    if n_sb < NB:
        score = jnp.where(blk < n_sb, score, KNOCKED_OUT)
    blk_f = blk.astype(F32)
    sel = forced.astype(F32)
    for _ in range(n_sel - 3):
        mx = jnp.max(score, axis=0, keepdims=True)
        first = jnp.min(jnp.where(score == mx, blk_f, float(NB)), axis=0, keepdims=True)
        hit = blk_f == first
        sel = jnp.where(hit, 1.0, sel)
        score = jnp.where(hit, KNOCKED_OUT, score)
    sel_past = jnp.where(blk < s0 // SLC_BLOCK, sel, 0.0)
    sel_bias = ((sel_past - 1.0) * (-NEG)).T.astype(BF16)
    qaug_ref[:, 0:HEAD_DIM] = q4
    for h in range(HPG):
        qaug_ref[h * Q_BLOCK:(h + 1) * Q_BLOCK, HEAD_DIM:D2] = sel_bias

    own = pl.ds(pl.multiple_of(s0, Q_BLOCK), Q_BLOCK)
    lower = (lax.broadcasted_iota(jnp.int32, (Q_BLOCK, Q_BLOCK), 1)
             <= lax.broadcasted_iota(jnp.int32, (Q_BLOCK, Q_BLOCK), 0))
    s_own = (_qk(q4, ks_ref[own, :]).reshape(HPG, Q_BLOCK, Q_BLOCK)
             + jnp.where(lower, 0.0, NEG)[None]).reshape(R, Q_BLOCK)
    m_own = jnp.broadcast_to(jnp.max(s_own, axis=-1, keepdims=True), (R, HEAD_DIM))
    mown_ref[...] = m_own
    own_ref[...] = jnp.dot(jnp.exp2(s_own - m_own).astype(BF16), vaug_ref[own, :],
                           preferred_element_type=F32)
    m_ref[...] = m_own
    acc_ref[...] = jnp.zeros((R, D2), F32)
    n_past = (s0 + tk - 1) // tk

    @pl.loop(0, n_past)
    def _(kt):
        k0 = pl.multiple_of(kt * tk, tk)
        s = _qk(qaug_ref[...], kaug_ref[pl.ds(k0, tk), :])
        s_ref[:, pl.ds(k0, tk)] = s
        m = m_ref[...]
        for c in range(tk // HEAD_DIM):
            m = jnp.maximum(m, s[:, c * HEAD_DIM:(c + 1) * HEAD_DIM])
        m_ref[...] = m

    m_ref[...] = jnp.broadcast_to(jnp.max(m_ref[...], axis=-1, keepdims=True), (R, HEAD_DIM))

    @pl.loop(0, n_past)
    def _(kt):
        k0 = pl.multiple_of(kt * tk, tk)
        m_b = jnp.concatenate([m_ref[...]] * (tk // HEAD_DIM), axis=1)
        p = jnp.exp2(s_ref[:, pl.ds(k0, tk)] - m_b).astype(BF16)
        acc_ref[...] += jnp.dot(p, vaug_ref[pl.ds(k0, tk), :], preferred_element_type=F32)

    w_own = jnp.exp2(mown_ref[...] - m_ref[...])
    acc_s = acc_ref[...] + jnp.concatenate([w_own, w_own], axis=1) * own_ref[...]
    o_slc = (acc_s[:, :HEAD_DIM] / jnp.maximum(acc_s[:, HEAD_DIM:], 1e-30)
             ).reshape(HPG, Q_BLOCK, HEAD_DIM)

    gt = g_ref[...]
    for h in range(HPG):
        cols = slice(h * HEAD_DIM, (h + 1) * HEAD_DIM)
        o_ref[:, cols] = (part_ref[:, cols]
                          + gt[:, HPG + h:HPG + h + 1] * o_slc[h]).astype(o_ref.dtype)


def _nsa(q, kc, vc, kv6, gates, *, B, S, tk=2048):
    N = B * S
    nq = S // Q_BLOCK
    nc = kc.shape[2]
    n_sb = S // SLC_BLOCK
    assert n_sb <= NB and nc % HEAD_DIM == 0 and S % tk == 0
    G = N_KV_GROUPS
    gw = HPG * HEAD_DIM
    kv_spec = lambda t: pl.BlockSpec((None, None, S, HEAD_DIM), lambda b, g, i, t=t: (t, g, b, 0))
    cmp_spec = pl.BlockSpec((None, None, nc, HEAD_DIM), lambda b, g, i: (b, g, 0, 0))
    return pl.pallas_call(
        functools.partial(_nsa_body, S=S, nc=nc, n_sb=n_sb, n_sel=min(N_SELECT, n_sb), tk=tk),
        grid=(B, G, nq),
        in_specs=[pl.BlockSpec((Q_BLOCK, gw), lambda b, g, i: (b * nq + i, g)),
                  cmp_spec, cmp_spec, kv_spec(2), kv_spec(3), kv_spec(4), kv_spec(5),
                  pl.BlockSpec((Q_BLOCK, 128), lambda b, g, i: (b * nq + i, g))],
        out_specs=pl.BlockSpec((Q_BLOCK, gw), lambda b, g, i: (b * nq + i, g)),
        out_shape=jax.ShapeDtypeStruct((N, G * gw), BF16),
        scratch_shapes=[pltpu.VMEM((S, 2 * HEAD_DIM), BF16), pltpu.VMEM((S, 2 * HEAD_DIM), BF16),
                        pltpu.VMEM((S, 2 * HEAD_DIM), BF16),
                        pltpu.VMEM((NB, nc), BF16),
                        pltpu.VMEM((HPG * Q_BLOCK, 2 * HEAD_DIM), BF16),
                        pltpu.VMEM((HPG * Q_BLOCK, HEAD_DIM), F32),
                        pltpu.VMEM((HPG * Q_BLOCK, 2 * HEAD_DIM), F32),
                        pltpu.VMEM((HPG * Q_BLOCK, S), F32),
                        pltpu.VMEM((Q_BLOCK, HPG * HEAD_DIM), F32),
                        pltpu.VMEM((HPG * Q_BLOCK, 2 * HEAD_DIM), F32),
                        pltpu.VMEM((HPG * Q_BLOCK, HEAD_DIM), F32)],
        compiler_params=_params(("parallel", "parallel", "arbitrary")),
        name="nsa_attention",
    )(q, kc, vc, kv6, kv6, kv6, kv6, gates)


def _mlp_body(x_ref, g_ref, wu_ref, wd_ref, o_ref, h_ref, acc_ref):
    f = pl.program_id(1)

    @pl.when(f == 0)
    def _():
        h_ref[...] = _rms(x_ref[...], g_ref[...]).astype(BF16)
        acc_ref[...] = jnp.zeros_like(acc_ref)

    up = jnp.dot(h_ref[...], wu_ref[...], preferred_element_type=F32)
    up = jnp.square(jnp.maximum(up, 0.0)).astype(BF16)
    acc_ref[...] += jnp.dot(up, wd_ref[...], preferred_element_type=F32)

    @pl.when(f == pl.num_programs(1) - 1)
    def _():
        o_ref[...] = x_ref[...] + acc_ref[...]


def _mlp(x, g, w_up, w_down, *, tm=512, tf=512):
    M, D = x.shape
    FF = w_up.shape[1]
    return pl.pallas_call(
        _mlp_body,
        grid=(M // tm, FF // tf),
        in_specs=[pl.BlockSpec((tm, D), lambda i, f: (i, 0)),
                  pl.BlockSpec((1, D), lambda i, f: (0, 0)),
                  pl.BlockSpec((D, tf), lambda i, f: (0, f)),
                  pl.BlockSpec((tf, D), lambda i, f: (f, 0))],
        out_specs=pl.BlockSpec((tm, D), lambda i, f: (i, 0)),
        out_shape=jax.ShapeDtypeStruct((M, D), F32),
        scratch_shapes=[pltpu.VMEM((tm, D), BF16), pltpu.VMEM((tm, D), F32)],
        compiler_params=_params(("parallel", "arbitrary")),
        name="mlp",
    )(x, g.reshape(1, D).astype(F32), w_up, w_down)


def _ple_body(x_ref, p_ref, g_ref, wg_ref, wp_ref, gf_ref, o_ref, *, final_norm):
    x = x_ref[...]
    h = _rms(x, g_ref[...]).astype(BF16)
    gate = jax.nn.sigmoid(jnp.dot(h, wg_ref[...], preferred_element_type=F32))
    proj = jnp.dot(p_ref[...].astype(BF16), wp_ref[...], preferred_element_type=F32)
    y = x + gate * proj
    o_ref[...] = _rms(y, gf_ref[...]) if final_norm else y


def _ple(x, p, g, w_gate, w_proj, g_final, *, final_norm, tm=512):
    M, D = x.shape
    P = p.shape[1]
    row = lambda v: v.reshape(1, D).astype(F32)
    vec = pl.BlockSpec((1, D), lambda i: (0, 0))
    return pl.pallas_call(
        functools.partial(_ple_body, final_norm=final_norm),
        grid=(M // tm,),
        in_specs=[pl.BlockSpec((tm, D), lambda i: (i, 0)), pl.BlockSpec((tm, P), lambda i: (i, 0)),
                  vec, pl.BlockSpec((D, D), lambda i: (0, 0)), pl.BlockSpec((P, D), lambda i: (0, 0)),
                  vec],
        out_specs=pl.BlockSpec((tm, D), lambda i: (i, 0)),
        out_shape=jax.ShapeDtypeStruct((M, D), F32),
        compiler_params=_params(("parallel",)),
        name="ple_final",
    )(x, p, row(g), w_gate, w_proj, row(g_final))


def _layer(x, p, norm_mix_g, w_in, b_in, conv_w, conv_b, w_gate_a, b_gate_a, w_gate_x, b_gate_x,
           lru_lambda, w_lru_out, cmp_pos_k, cmp_w1_k, cmp_w2_k, cmp_pos_v, cmp_w1_v, cmp_w2_v,
           w_attn_out, w_o, norm_mlp_g, w_up, w_down, norm_ple_g, w_ple_gate, w_ple_proj,
           norm_final_g, *, B, S, final_norm):
    N, D = x.shape
    C = conv_w.shape[1]
    G = N_KV_GROUPS
    QW = N_HEADS * HEAD_DIM
    KVW = G * HEAD_DIM
    o_ly, o_q, o_kv, o_g, o_m = C, 2 * C, 2 * C + QW, 2 * C + QW + 6 * KVW, 2 * C + QW + 6 * KVW + 3 * N_HEADS
    wb = lambda lo, hi: (w_in[:, lo:hi].astype(BF16), b_in[lo:hi].reshape(1, hi - lo).astype(F32))
    proj = functools.partial(_mm, _norm_cast(x, norm_mix_g))

    w, b = wb(0, o_ly)
    u_lx = proj(w, lambda acc, b: acc + b, [(b, 'row', 0)], out_dtype=F32, name="in_lru_x")
    w, b = wb(o_ly, o_q)
    gelu_y = proj(w, lambda acc, b: _gelu(acc + b), [(b, 'row', 0)], out_dtype=BF16, name="in_lru_y")
    w, b = wb(o_q, o_kv)
    q = proj(w, lambda acc, b: (acc + b) * (HEAD_DIM ** -0.5 * LOG2E), [(b, 'row', 0)],
             out_dtype=BF16, name="in_q")
    w, b = wb(o_kv, o_g)
    kv6 = proj(w, lambda acc, b: acc + b, [(b, 'row', 0)], out_dtype=BF16, tn=KVW, split=G,
               name="in_kv")
    wg = w_in[:, o_g:o_m].reshape(D, 3, G, HPG).transpose(0, 2, 1, 3).reshape(D, G, 3 * HPG)
    wg = jnp.pad(wg, ((0, 0), (0, 0), (0, 128 - 3 * HPG))).reshape(D, G * 128).astype(BF16)
    bg = b_in[o_g:o_m].reshape(3, G, HPG).transpose(1, 0, 2).reshape(G, 3 * HPG)
    bg = jnp.pad(bg, ((0, 0), (0, 128 - 3 * HPG))).reshape(1, G * 128).astype(F32)
    gates = proj(wg, lambda acc, b: jax.nn.sigmoid(acc + b), [(bg, 'row', 0)], out_dtype=F32,
                 name="in_gates")
    w, b = wb(o_m, o_m + 2 * D)
    gm = proj(w, lambda acc, b: jax.nn.sigmoid(acc + b), [(b, 'row', 0)], out_dtype=BF16,
              name="in_merge_gates")

    a_in = _rg_lru(u_lx, gelu_y, conv_w, conv_b, w_gate_a, b_gate_a, w_gate_x, b_gate_x,
                   lru_lambda, B=B, S=S)
    nc = S // CMP_STRIDE
    kv_rows = kv6.reshape(6, G, B * nc, CMP_STRIDE * HEAD_DIM)
    kc = _compress(kv_rows, 0, cmp_pos_k, cmp_w1_k, cmp_w2_k, B=B, name="compress_k")
    vc = _compress(kv_rows, 1, cmp_pos_v, cmp_w1_v, cmp_w2_v, B=B, name="compress_v")
    o_nsa = _nsa(q, kc, vc, kv6, gates, B=B, S=S)

    nt = D // 512
    ya = _mm(a_in, w_lru_out.astype(BF16), lambda acc, ga: acc * ga.astype(F32),
             [(gm, 'tile', 0)], out_dtype=BF16, name="lru_out")
    mixed = _mm(o_nsa, w_attn_out.astype(BF16),
                lambda acc, ya, gb: ya.astype(F32) + acc * gb.astype(F32),
                [(ya, 'tile', 0), (gm, 'tile', nt)], out_dtype=BF16, name="attn_out_merge")
    x1 = _mm(mixed, w_o.astype(BF16), lambda acc, xr: xr + acc, [(x, 'tile', 0)],
             out_dtype=F32, name="out_proj")
    x2 = _mlp(x1, norm_mlp_g, w_up.astype(BF16), w_down.astype(BF16))
    return _ple(x2, p, norm_ple_g, w_ple_gate.astype(BF16), w_ple_proj.astype(BF16),
                norm_final_g, final_norm=final_norm)


def kernel(x, p, norm_mix_g, w_in, b_in, conv_w, conv_b, w_gate_a, b_gate_a, w_gate_x, b_gate_x, lru_lambda, w_lru_out, cmp_pos_k, cmp_w1_k, cmp_w2_k, cmp_pos_v, cmp_w1_v, cmp_w2_v, w_attn_out, w_o, norm_mlp_g, w_up, w_down, norm_ple_g, w_ple_gate, w_ple_proj, norm_final_g):
    B, S, D = x.shape
    depth = w_in.shape[0]
    assert S % 512 == 0 and S >= WINDOW + Q_BLOCK and D % 512 == 0
    xs = x.reshape(B * S, D)
    for i in range(depth):
        per_layer = [t[i] for t in (
            norm_mix_g, w_in, b_in, conv_w, conv_b, w_gate_a, b_gate_a, w_gate_x, b_gate_x,
            lru_lambda, w_lru_out, cmp_pos_k, cmp_w1_k, cmp_w2_k, cmp_pos_v, cmp_w1_v, cmp_w2_v,
            w_attn_out, w_o, norm_mlp_g, w_up, w_down, norm_ple_g, w_ple_gate, w_ple_proj)]
        xs = _layer(xs, p[i].reshape(B * S, -1), *per_layer, norm_final_g, B=B, S=S,
                    final_norm=i == depth - 1)
    return xs.reshape(B, S, D)
```

```python
import functools

import jax
import jax.numpy as jnp
from jax import lax
from jax.experimental import pallas as pl
from jax.experimental.pallas import tpu as pltpu

F32 = jnp.float32
BF16 = jnp.bfloat16

LRU_BLOCK_W = 128
CONV_WIDTH = 4
LRU_C = 8.0
N_HEADS = 16
N_KV_GROUPS = 4
HPG = N_HEADS // N_KV_GROUPS
HEAD_DIM = 128
CMP_STRIDE = 16
CMP_BLOCK = 32
SLC_BLOCK = 64
N_SELECT = 16
WINDOW = 512
Q_BLOCK = 128
EPS = 1e-6
NEG = -1e30
FORCED_SCORE = 1e4
KNOCKED_OUT = -3e38
NB = 128
LOG2E = 1.4426950408889634

VMEM_LIMIT = 56 * 1024 * 1024


def _params(sem):
    return pltpu.CompilerParams(dimension_semantics=sem, vmem_limit_bytes=VMEM_LIMIT)


def _rms(xf, g):
    return xf * lax.rsqrt(jnp.mean(xf * xf, axis=-1, keepdims=True) + EPS) * g


def _gelu(x):
    return jax.nn.gelu(x)


def _norm_body(x_ref, g_ref, o_ref):
    o_ref[...] = _rms(x_ref[...], g_ref[...]).astype(o_ref.dtype)


def _norm_cast(x, g, *, tm=1024):
    M, D = x.shape
    return pl.pallas_call(
        _norm_body,
        grid=(M // tm,),
        in_specs=[pl.BlockSpec((tm, D), lambda i: (i, 0)), pl.BlockSpec((1, D), lambda i: (0, 0))],
        out_specs=pl.BlockSpec((tm, D), lambda i: (i, 0)),
        out_shape=jax.ShapeDtypeStruct((M, D), BF16),
        compiler_params=_params(("parallel",)),
        name="norm_mix",
    )(x, g.reshape(1, D).astype(F32))


def _mm_body(*refs, n_extra, epi, split):
    a_ref, w_ref = refs[0], refs[1]
    extras = refs[2:2 + n_extra]
    o_ref = refs[2 + n_extra]
    acc = jnp.dot(a_ref[...], w_ref[...], preferred_element_type=F32)
    res = epi(acc, *[e[...] for e in extras])
    if split:
        for g in range(split):
            o_ref[g] = res[:, g * HEAD_DIM:(g + 1) * HEAD_DIM].astype(o_ref.dtype)
    else:
        o_ref[...] = res.astype(o_ref.dtype)


def _mm(a, w, epi, extras=(), *, out_dtype, tm=1024, tn=512, split=0, name):
    M, K = a.shape
    N = w.shape[1]
    tn = min(tn, N)
    grid = (M // tm, N // tn)
    in_specs = [pl.BlockSpec((tm, K), lambda i, j: (i, 0)),
                pl.BlockSpec((K, tn), lambda i, j: (0, j))]
    args = [a, w]
    for arr, kind, off in extras:
        if kind == 'row':
            in_specs.append(pl.BlockSpec((1, tn), lambda i, j, off=off: (0, j + off)))
        else:
            in_specs.append(pl.BlockSpec((tm, tn), lambda i, j, off=off: (i, j + off)))
        args.append(arr)
    if split:
        out_shape = jax.ShapeDtypeStruct((N // tn, split, M, HEAD_DIM), out_dtype)
        out_spec = pl.BlockSpec((None, split, tm, HEAD_DIM), lambda i, j: (j, 0, i, 0))
    else:
        out_shape = jax.ShapeDtypeStruct((M, N), out_dtype)
        out_spec = pl.BlockSpec((tm, tn), lambda i, j: (i, j))
    return pl.pallas_call(
        functools.partial(_mm_body, n_extra=len(extras), epi=epi, split=split),
        grid=grid, in_specs=in_specs, out_specs=out_spec, out_shape=out_shape,
        compiler_params=_params(("parallel", "arbitrary")),
        name=name,
    )(*args)


def _merge_body(a_ref, b_ref, wa_ref, wb_ref, ga_ref, gb_ref, o_ref):
    ya = jnp.dot(a_ref[...], wa_ref[...], preferred_element_type=F32)
    yb = jnp.dot(b_ref[...], wb_ref[...], preferred_element_type=F32)
    o_ref[...] = (ga_ref[...].astype(F32) * ya + gb_ref[...].astype(F32) * yb).astype(o_ref.dtype)


def _merge(a, b, wa, wb, gates, *, tm=1024, tn=512):
    M, K = a.shape
    N = wa.shape[1]
    nt = N // tn
    lhs = pl.BlockSpec((tm, K), lambda i, j: (i, 0))
    wsp = pl.BlockSpec((K, tn), lambda i, j: (0, j))
    return pl.pallas_call(
        _merge_body,
        grid=(M // tm, nt),
        in_specs=[lhs, lhs, wsp, wsp, pl.BlockSpec((tm, tn), lambda i, j: (i, j)),
                  pl.BlockSpec((tm, tn), lambda i, j: (i, j + nt))],
        out_specs=pl.BlockSpec((tm, tn), lambda i, j: (i, j)),
        out_shape=jax.ShapeDtypeStruct((M, N), BF16),
        compiler_params=_params(("parallel", "arbitrary")),
        name="mixer_merge",
    )(a, b, wa, wb, gates, gates)


def _lru_body(u_ref, gy_ref, cw_ref, cb_ref, wa_ref, ba_ref, wx_ref, bx_ref, lam_ref,
              o_ref, xe_ref, xc_ref, a_ref, b_ref, hn_ref, h_ref, *, T, ct):
    L = T // 8
    W = LRU_BLOCK_W
    nb = ct // W
    s = pl.program_id(2)

    @pl.when(s == 0)
    def _():
        xe_ref[:, 0:8, :] = jnp.zeros((nb, 8, W), F32)
        h_ref[...] = jnp.zeros((nb, 8, W), F32)

    lam = lam_ref[...]
    neg_softplus = -(jnp.maximum(-lam, 0.0) + jnp.log1p(jnp.exp(-jnp.abs(lam))))
    log_a_scale = LRU_C * neg_softplus
    for n in range(nb):
        sl = slice(n * W, (n + 1) * W)
        xe_ref[n, 8:, :] = u_ref[:, sl]

        def seg_rows(off):
            return xe_ref[n, pl.ds(off, 8, stride=L), :]

        taps = [jnp.broadcast_to(cw_ref[k:k + 1, sl], (8, W)) for k in range(CONV_WIDTH)]
        bias = jnp.broadcast_to(cb_ref[:, sl], (8, W))
        prev = [seg_rows(8 - CONV_WIDTH + 1 + k) for k in range(CONV_WIDTH - 1)]
        for t in range(L):
            win = prev + [seg_rows(8 + t)]
            y = bias
            for k in range(CONV_WIDTH):
                y = y + taps[k] * win[k]
            xc_ref[t * 8:(t + 1) * 8, :] = y
            prev = win[1:]
        xe_ref[n, 0:8, :] = xe_ref[n, T:T + 8, :]

        xc = xc_ref[...]
        xb = xc.astype(BF16)
        r = jax.nn.sigmoid(jnp.dot(xb, wa_ref[n], preferred_element_type=F32) + ba_ref[:, sl])
        i = jax.nn.sigmoid(jnp.dot(xb, wx_ref[n], preferred_element_type=F32) + bx_ref[:, sl])
        a = jnp.exp(r * log_a_scale[:, sl])
        a_ref[n] = a
        b_ref[n] = jnp.sqrt(jnp.maximum(1.0 - a * a, 0.0)) * (i * xc)

    def step(t, carry):
        rows = pl.ds(pl.multiple_of(t * 8, 8), 8)
        out = []
        for n in range(nb):
            h, decay = carry[n]
            a_t = a_ref[n, rows, :]
            h = a_t * h + b_ref[n, rows, :]
            decay = a_t * decay
            b_ref[n, rows, :] = h
            a_ref[n, rows, :] = decay
            out.append((h, decay))
        return tuple(out)

    init = tuple((jnp.zeros((8, W), F32), jnp.ones((8, W), F32)) for _ in range(nb))
    ends = lax.fori_loop(0, L, step, init, unroll=8)
    row = lax.broadcasted_iota(jnp.int32, (8, W), 0)
    for n in range(nb):
        h_end, a_end = ends[n]
        h_in = h_ref[n]
        for r in range(1, 8):
            h_in = jnp.where(row == r, pltpu.roll(a_end * h_in + h_end, 1, axis=0), h_in)
        h_ref[n] = jnp.broadcast_to((a_end * h_in + h_end)[7:8, :], (8, W))
        for t in range(L):
            rows = slice(t * 8, (t + 1) * 8)
            hn_ref[pl.ds(t, 8, stride=L), :] = b_ref[n, rows, :] + a_ref[n, rows, :] * h_in
        sl = slice(n * W, (n + 1) * W)
        o_ref[:, sl] = (hn_ref[...] * gy_ref[:, sl].astype(F32)).astype(o_ref.dtype)


def _rg_lru(u_lx, gelu_y, conv_w, conv_b, w_a, b_a, w_x, b_x, lam, *, B, S, T=512, ct=512):
    N, C = u_lx.shape
    nb = ct // LRU_BLOCK_W
    ns = S // T
    row = lambda v: v.reshape(1, C).astype(F32)
    tile = pl.BlockSpec((T, ct), lambda b, c, s: (b * ns + s, c))
    vec = pl.BlockSpec((1, ct), lambda b, c, s: (0, c))
    wblk = pl.BlockSpec((nb, LRU_BLOCK_W, LRU_BLOCK_W), lambda b, c, s: (c, 0, 0))
    return pl.pallas_call(
        functools.partial(_lru_body, T=T, ct=ct),
        grid=(B, C // ct, ns),
        in_specs=[tile, tile, pl.BlockSpec((CONV_WIDTH, ct), lambda b, c, s: (0, c)), vec,
                  wblk, vec, wblk, vec, vec],
        out_specs=tile,
        out_shape=jax.ShapeDtypeStruct((N, C), BF16),
        scratch_shapes=[pltpu.VMEM((nb, T + 8, LRU_BLOCK_W), F32), pltpu.VMEM((T, LRU_BLOCK_W), F32),
                        pltpu.VMEM((nb, T, LRU_BLOCK_W), F32), pltpu.VMEM((nb, T, LRU_BLOCK_W), F32),
                        pltpu.VMEM((T, LRU_BLOCK_W), F32), pltpu.VMEM((nb, 8, LRU_BLOCK_W), F32)],
        compiler_params=_params(("parallel", "parallel", "arbitrary")),
        name="rg_lru",
    )(u_lx, gelu_y, conv_w.astype(F32), row(conv_b), w_a.astype(BF16), row(b_a),
      w_x.astype(BF16), row(b_x), row(lam))


def _cmp_body(x_ref, pos_ref, w1_ref, w2_ref, o_ref, *, nc, half):
    x = x_ref[...]
    first = jnp.dot(x, w1_ref[0:half, :], preferred_element_type=F32)
    second = jnp.dot(x, w1_ref[half:, :], preferred_element_type=F32)
    pos = jnp.broadcast_to(pos_ref[...], (8, 2 * half)).astype(BF16)
    cpos = jnp.dot(pos, w1_ref[...], preferred_element_type=F32)[0:1, :]
    hid = first + pltpu.roll(second, nc - 1, axis=0) + cpos
    o_ref[...] = jnp.dot(_gelu(hid).astype(BF16), w2_ref[...],
                         preferred_element_type=F32).astype(o_ref.dtype)


def _compress(x, pos, w1, w2, *, B, name):
    G, rows, half = x.shape
    nc = rows // B
    hidden = w1.shape[1]
    return pl.pallas_call(
        functools.partial(_cmp_body, nc=nc, half=half),
        grid=(B, G),
        in_specs=[pl.BlockSpec((None, nc, half), lambda b, g: (g, b, 0)),
                  pl.BlockSpec((1, 2 * half), lambda b, g: (0, 0)),
                  pl.BlockSpec((2 * half, hidden), lambda b, g: (0, 0)),
                  pl.BlockSpec((hidden, HEAD_DIM), lambda b, g: (0, 0))],
        out_specs=pl.BlockSpec((None, None, nc, HEAD_DIM), lambda b, g: (b, g, 0, 0)),
        out_shape=jax.ShapeDtypeStruct((B, G, nc, HEAD_DIM), BF16),
        compiler_params=_params(("parallel", "parallel")),
        name=name,
    )(x, pos.reshape(1, 2 * half).astype(F32), w1.astype(BF16), w2.astype(BF16))


def _qk(q, k):
    return lax.dot_general(q, k, (((1,), (1,)), ((), ())), preferred_element_type=F32)


def _nsa_body(q_ref, kc_ref, vc_ref, ks_ref, vs_ref, kw_ref, vw_ref, g_ref, o_ref,
              kaug_ref, vaug_ref, vwaug_ref, ovt_ref, qaug_ref, m_ref, acc_ref,
              sbuf_ref, part_ref, *, S, nc, n_sb, n_sel, tk):
    R = HPG * Q_BLOCK
    D2 = 2 * HEAD_DIM
    i = pl.program_id(2)
    s0 = i * Q_BLOCK

    @pl.when(i == 0)
    def _():
        key_blk = lax.broadcasted_iota(jnp.int32, (S, NB), 0) // SLC_BLOCK
        col = lax.broadcasted_iota(jnp.int32, (S, NB), 1)
        kaug_ref[:, 0:HEAD_DIM] = ks_ref[...]
        kaug_ref[:, HEAD_DIM:D2] = (key_blk == col).astype(BF16)
        vaug_ref[:, 0:HEAD_DIM] = vs_ref[...]
        vaug_ref[:, HEAD_DIM:D2] = jnp.ones((S, HEAD_DIM), BF16)
        vwaug_ref[:, 0:HEAD_DIM] = vw_ref[...]
        vwaug_ref[:, HEAD_DIM:D2] = jnp.ones((S, HEAD_DIM), BF16)
        ci = lax.broadcasted_iota(jnp.int32, (NB, nc), 1) * CMP_STRIDE
        si = lax.broadcasted_iota(jnp.int32, (NB, nc), 0) * SLC_BLOCK
        ovt_ref[...] = ((ci < si + SLC_BLOCK) & (si < ci + CMP_BLOCK)).astype(BF16)

    qb = q_ref[...]
    q4 = jnp.concatenate([qb[:, h * HEAD_DIM:(h + 1) * HEAD_DIM] for h in range(HPG)], axis=0)
    tq = s0 + lax.broadcasted_iota(jnp.int32, (Q_BLOCK, 1), 0)

    def softmax_pv(s, bias, v_aug):
        k = s.shape[-1]
        s3 = s.reshape(HPG, Q_BLOCK, k) + bias[None]
        e = jnp.exp2(s3 - jnp.max(s3, axis=-1, keepdims=True))
        return e, jnp.dot(e.reshape(R, k).astype(BF16), v_aug, preferred_element_type=F32)

    cend = lax.broadcasted_iota(jnp.int32, (1, nc), 1) * CMP_STRIDE + (CMP_BLOCK - 1)
    e_c, oc = softmax_pv(_qk(q4, kc_ref[...]), jnp.where(cend <= tq, 0.0, NEG), vc_ref[...])
    inv_c = 1.0 / jnp.maximum(jnp.sum(e_c, axis=-1, keepdims=True), 1e-30)
    any_c = (tq >= CMP_BLOCK - 1).astype(F32)
    o_cmp = oc.reshape(HPG, Q_BLOCK, HEAD_DIM) * (inv_c * any_c[None])

    span = WINDOW + Q_BLOCK
    w0 = pl.multiple_of(jnp.maximum(s0 - WINDOW, 0), Q_BLOCK)
    diff = tq - (w0 + lax.broadcasted_iota(jnp.int32, (1, span), 1))
    _, ow = softmax_pv(_qk(q4, kw_ref[pl.ds(w0, span), :]),
                       jnp.where((diff >= 0) & (diff < WINDOW), 0.0, NEG),
                       vwaug_ref[pl.ds(w0, span), :])
    o_win = (ow[:, :HEAD_DIM] / jnp.maximum(ow[:, HEAD_DIM:], 1e-30)
             ).reshape(HPG, Q_BLOCK, HEAD_DIM)
    gt = g_ref[...]
    for h in range(HPG):
        part_ref[:, h * HEAD_DIM:(h + 1) * HEAD_DIM] = (
            gt[:, h:h + 1] * o_cmp[h] + gt[:, 2 * HPG + h:2 * HPG + h + 1] * o_win[h])

    pc_sum = e_c[0] * inv_c[0]
    for h in range(1, HPG):
        pc_sum = pc_sum + e_c[h] * inv_c[h]
    pc_hi = pc_sum.astype(BF16)
    pc_lo = (pc_sum - pc_hi.astype(F32)).astype(BF16)
    ovt = ovt_ref[...]
    imp = _qk(ovt, pc_hi) + _qk(ovt, pc_lo)
    blk = lax.broadcasted_iota(jnp.int32, (NB, 1), 0)
    cur = (s0 + lax.broadcasted_iota(jnp.int32, (1, Q_BLOCK), 1)) // SLC_BLOCK
    forced = (blk == 0) | (blk == cur) | (blk == cur - 1)
    score = jnp.where(forced, KNOCKED_OUT, jnp.where(blk <= cur, imp, -FORCED_SCORE))
    if n_sb < NB:
        score = jnp.where(blk < n_sb, score, KNOCKED_OUT)
    blk_f = blk.astype(F32)
    sel = forced.astype(F32)
    for _ in range(n_sel - 3):
        mx = jnp.max(score, axis=0, keepdims=True)
        first = jnp.min(jnp.where(score == mx, blk_f, float(NB)), axis=0, keepdims=True)
        hit = blk_f == first
        sel = jnp.where(hit, 1.0, sel)
        score = jnp.where(hit, KNOCKED_OUT, score)
    sel_past = jnp.where(blk < s0 // SLC_BLOCK, sel, 0.0)
    sel_bias = ((sel_past - 1.0) * (-NEG)).T.astype(BF16)
    qaug_ref[:, 0:HEAD_DIM] = q4
    for h in range(HPG):
        qaug_ref[h * Q_BLOCK:(h + 1) * Q_BLOCK, HEAD_DIM:D2] = sel_bias

    own = pl.ds(pl.multiple_of(s0, Q_BLOCK), Q_BLOCK)
    lower = (lax.broadcasted_iota(jnp.int32, (Q_BLOCK, Q_BLOCK), 1)
             <= lax.broadcasted_iota(jnp.int32, (Q_BLOCK, Q_BLOCK), 0))
    s_own = (_qk(q4, ks_ref[own, :]).reshape(HPG, Q_BLOCK, Q_BLOCK)
             + jnp.where(lower, 0.0, NEG)[None]).reshape(R, Q_BLOCK)
    m_own = jnp.broadcast_to(jnp.max(s_own, axis=-1, keepdims=True), (R, HEAD_DIM))
    m_ref[...] = m_own
    acc_ref[...] = jnp.dot(jnp.exp2(s_own - m_own).astype(BF16), vaug_ref[own, :],
                           preferred_element_type=F32)
    n_past = (s0 + tk - 1) // tk

    def scores(kt):
        return _qk(qaug_ref[...], kaug_ref[pl.ds(pl.multiple_of(kt * tk, tk), tk), :])

    def consume(kt):
        s = sbuf_ref[...]
        m_old = m_ref[...]
        m_new = jnp.maximum(m_old, jnp.max(s, axis=-1, keepdims=True))
        p = jnp.exp2(s - jnp.concatenate([m_new] * (tk // HEAD_DIM), axis=1)).astype(BF16)
        alpha = jnp.exp2(m_old - m_new)
        m_ref[...] = m_new
        acc_ref[...] = (jnp.concatenate([alpha, alpha], axis=1) * acc_ref[...]
                        + jnp.dot(p, vaug_ref[pl.ds(pl.multiple_of(kt * tk, tk), tk), :],
                                  preferred_element_type=F32))

    sbuf_ref[...] = scores(0)

    @pl.loop(0, jnp.maximum(n_past - 1, 0))
    def _(kt):
        s_next = scores(kt + 1)
        consume(kt)
        sbuf_ref[...] = s_next

    consume(jnp.maximum(n_past - 1, 0))
    acc_s = acc_ref[...]
    o_slc = (acc_s[:, :HEAD_DIM] / jnp.maximum(acc_s[:, HEAD_DIM:], 1e-30)
             ).reshape(HPG, Q_BLOCK, HEAD_DIM)

    gt = g_ref[...]
    for h in range(HPG):
        cols = slice(h * HEAD_DIM, (h + 1) * HEAD_DIM)
        o_ref[:, cols] = (part_ref[:, cols]
                          + gt[:, HPG + h:HPG + h + 1] * o_slc[h]).astype(o_ref.dtype)


def _nsa(q, kc, vc, kv6, gates, *, B, S, tk=2048):
    N = B * S
    nq = S // Q_BLOCK
    nc = kc.shape[2]
    n_sb = S // SLC_BLOCK
    assert n_sb <= NB and nc % HEAD_DIM == 0 and S % tk == 0
    G = N_KV_GROUPS
    gw = HPG * HEAD_DIM
    kv_spec = lambda t: pl.BlockSpec((None, None, S, HEAD_DIM), lambda b, g, i, t=t: (t, g, b, 0))
    cmp_spec = pl.BlockSpec((None, None, nc, HEAD_DIM), lambda b, g, i: (b, g, 0, 0))
    return pl.pallas_call(
        functools.partial(_nsa_body, S=S, nc=nc, n_sb=n_sb, n_sel=min(N_SELECT, n_sb), tk=tk),
        grid=(B, G, nq),
        in_specs=[pl.BlockSpec((Q_BLOCK, gw), lambda b, g, i: (b * nq + i, g)),
                  cmp_spec, cmp_spec, kv_spec(2), kv_spec(3), kv_spec(4), kv_spec(5),
                  pl.BlockSpec((Q_BLOCK, 128), lambda b, g, i: (b * nq + i, g))],
        out_specs=pl.BlockSpec((Q_BLOCK, gw), lambda b, g, i: (b * nq + i, g)),
        out_shape=jax.ShapeDtypeStruct((N, G * gw), BF16),
        scratch_shapes=[pltpu.VMEM((S, 2 * HEAD_DIM), BF16), pltpu.VMEM((S, 2 * HEAD_DIM), BF16),
                        pltpu.VMEM((S, 2 * HEAD_DIM), BF16),
                        pltpu.VMEM((NB, nc), BF16),
                        pltpu.VMEM((HPG * Q_BLOCK, 2 * HEAD_DIM), BF16),
                        pltpu.VMEM((HPG * Q_BLOCK, HEAD_DIM), F32),
                        pltpu.VMEM((HPG * Q_BLOCK, 2 * HEAD_DIM), F32),
                        pltpu.VMEM((HPG * Q_BLOCK, tk), F32),
                        pltpu.VMEM((Q_BLOCK, HPG * HEAD_DIM), F32)],
        compiler_params=_params(("parallel", "parallel", "arbitrary")),
        name="nsa_attention",
    )(q, kc, vc, kv6, kv6, kv6, kv6, gates)


def _mlp_body(x_ref, g_ref, wu_ref, wd_ref, o_ref, h_ref, acc_ref):
    f = pl.program_id(1)

    @pl.when(f == 0)
    def _():
        h_ref[...] = _rms(x_ref[...], g_ref[...]).astype(BF16)
        acc_ref[...] = jnp.zeros_like(acc_ref)

    up = jnp.dot(h_ref[...], wu_ref[...], preferred_element_type=F32)
    up = jnp.square(jnp.maximum(up, 0.0)).astype(BF16)
    acc_ref[...] += jnp.dot(up, wd_ref[...], preferred_element_type=F32)

    @pl.when(f == pl.num_programs(1) - 1)
    def _():
        o_ref[...] = x_ref[...] + acc_ref[...]


def _mlp(x, g, w_up, w_down, *, tm=512, tf=512):
    M, D = x.shape
    FF = w_up.shape[1]
    return pl.pallas_call(
        _mlp_body,
        grid=(M // tm, FF // tf),
        in_specs=[pl.BlockSpec((tm, D), lambda i, f: (i, 0)),
                  pl.BlockSpec((1, D), lambda i, f: (0, 0)),
                  pl.BlockSpec((D, tf), lambda i, f: (0, f)),
                  pl.BlockSpec((tf, D), lambda i, f: (f, 0))],
        out_specs=pl.BlockSpec((tm, D), lambda i, f: (i, 0)),
        out_shape=jax.ShapeDtypeStruct((M, D), F32),
        scratch_shapes=[pltpu.VMEM((tm, D), BF16), pltpu.VMEM((tm, D), F32)],
        compiler_params=_params(("parallel", "arbitrary")),
        name="mlp",
    )(x, g.reshape(1, D).astype(F32), w_up, w_down)


def _ple_body(x_ref, p_ref, g_ref, wg_ref, wp_ref, gf_ref, o_ref, *, final_norm):
    x = x_ref[...]
    h = _rms(x, g_ref[...]).astype(BF16)
    gate = jax.nn.sigmoid(jnp.dot(h, wg_ref[...], preferred_element_type=F32))
    proj = jnp.dot(p_ref[...].astype(BF16), wp_ref[...], preferred_element_type=F32)
    y = x + gate * proj
    o_ref[...] = _rms(y, gf_ref[...]) if final_norm else y


def _ple(x, p, g, w_gate, w_proj, g_final, *, final_norm, tm=512):
    M, D = x.shape
    P = p.shape[1]
    row = lambda v: v.reshape(1, D).astype(F32)
    vec = pl.BlockSpec((1, D), lambda i: (0, 0))
    return pl.pallas_call(
        functools.partial(_ple_body, final_norm=final_norm),
        grid=(M // tm,),
        in_specs=[pl.BlockSpec((tm, D), lambda i: (i, 0)), pl.BlockSpec((tm, P), lambda i: (i, 0)),
                  vec, pl.BlockSpec((D, D), lambda i: (0, 0)), pl.BlockSpec((P, D), lambda i: (0, 0)),
                  vec],
        out_specs=pl.BlockSpec((tm, D), lambda i: (i, 0)),
        out_shape=jax.ShapeDtypeStruct((M, D), F32),
        compiler_params=_params(("parallel",)),
        name="ple_final",
    )(x, p, row(g), w_gate, w_proj, row(g_final))


def _layer(x, p, norm_mix_g, w_in, b_in, conv_w, conv_b, w_gate_a, b_gate_a, w_gate_x, b_gate_x,
           lru_lambda, w_lru_out, cmp_pos_k, cmp_w1_k, cmp_w2_k, cmp_pos_v, cmp_w1_v, cmp_w2_v,
           w_attn_out, w_o, norm_mlp_g, w_up, w_down, norm_ple_g, w_ple_gate, w_ple_proj,
           norm_final_g, *, B, S, final_norm):
    N, D = x.shape
    C = conv_w.shape[1]
    G = N_KV_GROUPS
    QW = N_HEADS * HEAD_DIM
    KVW = G * HEAD_DIM
    o_ly, o_q, o_kv, o_g, o_m = C, 2 * C, 2 * C + QW, 2 * C + QW + 6 * KVW, 2 * C + QW + 6 * KVW + 3 * N_HEADS
    wb = lambda lo, hi: (w_in[:, lo:hi].astype(BF16), b_in[lo:hi].reshape(1, hi - lo).astype(F32))
    proj = functools.partial(_mm, _norm_cast(x, norm_mix_g))

    w, b = wb(0, o_ly)
    u_lx = proj(w, lambda acc, b: acc + b, [(b, 'row', 0)], out_dtype=F32, name="in_lru_x")
    w, b = wb(o_ly, o_q)
    gelu_y = proj(w, lambda acc, b: _gelu(acc + b), [(b, 'row', 0)], out_dtype=BF16, name="in_lru_y")
    w, b = wb(o_q, o_kv)
    q = proj(w, lambda acc, b: (acc + b) * (HEAD_DIM ** -0.5 * LOG2E), [(b, 'row', 0)],
             out_dtype=BF16, name="in_q")
    w, b = wb(o_kv, o_g)
    kv6 = proj(w, lambda acc, b: acc + b, [(b, 'row', 0)], out_dtype=BF16, tn=KVW, split=G,
               name="in_kv")
    wg = w_in[:, o_g:o_m].reshape(D, 3, G, HPG).transpose(0, 2, 1, 3).reshape(D, G, 3 * HPG)
    wg = jnp.pad(wg, ((0, 0), (0, 0), (0, 128 - 3 * HPG))).reshape(D, G * 128).astype(BF16)
    bg = b_in[o_g:o_m].reshape(3, G, HPG).transpose(1, 0, 2).reshape(G, 3 * HPG)
    bg = jnp.pad(bg, ((0, 0), (0, 128 - 3 * HPG))).reshape(1, G * 128).astype(F32)
    gates = proj(wg, lambda acc, b: jax.nn.sigmoid(acc + b), [(bg, 'row', 0)], out_dtype=F32,
                 name="in_gates")
    w, b = wb(o_m, o_m + 2 * D)
    gm = proj(w, lambda acc, b: jax.nn.sigmoid(acc + b), [(b, 'row', 0)], out_dtype=BF16,
              name="in_merge_gates")

    a_in = _rg_lru(u_lx, gelu_y, conv_w, conv_b, w_gate_a, b_gate_a, w_gate_x, b_gate_x,
                   lru_lambda, B=B, S=S)
    nc = S // CMP_STRIDE
    kc = _compress(kv6[0].reshape(G, B * nc, CMP_STRIDE * HEAD_DIM), cmp_pos_k, cmp_w1_k, cmp_w2_k,
                   B=B, name="compress_k")
    vc = _compress(kv6[1].reshape(G, B * nc, CMP_STRIDE * HEAD_DIM), cmp_pos_v, cmp_w1_v, cmp_w2_v,
                   B=B, name="compress_v")
    o_nsa = _nsa(q, kc, vc, kv6, gates, B=B, S=S)

    mixed = _merge(a_in, o_nsa, w_lru_out.astype(BF16), w_attn_out.astype(BF16), gm)
    x1 = _mm(mixed, w_o.astype(BF16), lambda acc, xr: xr + acc, [(x, 'tile', 0)],
             out_dtype=F32, name="out_proj")
    x2 = _mlp(x1, norm_mlp_g, w_up.astype(BF16), w_down.astype(BF16))
    return _ple(x2, p, norm_ple_g, w_ple_gate.astype(BF16), w_ple_proj.astype(BF16),
                norm_final_g, final_norm=final_norm)


def kernel(x, p, norm_mix_g, w_in, b_in, conv_w, conv_b, w_gate_a, b_gate_a, w_gate_x, b_gate_x, lru_lambda, w_lru_out, cmp_pos_k, cmp_w1_k, cmp_w2_k, cmp_pos_v, cmp_w1_v, cmp_w2_v, w_attn_out, w_o, norm_mlp_g, w_up, w_down, norm_ple_g, w_ple_gate, w_ple_proj, norm_final_g):
    B, S, D = x.shape
    depth = w_in.shape[0]
    assert S % 512 == 0 and S >= WINDOW + Q_BLOCK and D % 512 == 0
    xs = x.reshape(B * S, D)
    for i in range(depth):
        per_layer = [t[i] for t in (
            norm_mix_g, w_in, b_in, conv_w, conv_b, w_gate_a, b_gate_a, w_gate_x, b_gate_x,
            lru_lambda, w_lru_out, cmp_pos_k, cmp_w1_k, cmp_w2_k, cmp_pos_v, cmp_w1_v, cmp_w2_v,
            w_attn_out, w_o, norm_mlp_g, w_up, w_down, norm_ple_g, w_ple_gate, w_ple_proj)]
        xs = _layer(xs, p[i].reshape(B * S, -1), *per_layer, norm_final_g, B=B, S=S,
                    final_norm=i == depth - 1)
    return xs.reshape(B, S, D)
```

```python
import functools

import jax
import jax.numpy as jnp
from jax import lax
from jax.experimental import pallas as pl
from jax.experimental.pallas import tpu as pltpu

F32 = jnp.float32
BF16 = jnp.bfloat16

LRU_BLOCK_W = 128
CONV_WIDTH = 4
LRU_C = 8.0
N_HEADS = 16
N_KV_GROUPS = 4
HPG = N_HEADS // N_KV_GROUPS
HEAD_DIM = 128
CMP_STRIDE = 16
CMP_BLOCK = 32
SLC_BLOCK = 64
N_SELECT = 16
WINDOW = 512
Q_TILE = 256
EPS = 1e-6
NEG = -1e30
FORCED_SCORE = 1e4
KNOCKED_OUT = -3e38
NB = 128
LOG2E = 1.4426950408889634

VMEM_LIMIT = 56 * 1024 * 1024


def _params(sem):
    return pltpu.CompilerParams(dimension_semantics=sem, vmem_limit_bytes=VMEM_LIMIT)


def _rms(xf, g):
    return xf * lax.rsqrt(jnp.mean(xf * xf, axis=-1, keepdims=True) + EPS) * g


def _gelu(x):
    return jax.nn.gelu(x)


def _norm_body(x_ref, g_ref, o_ref):
    o_ref[...] = _rms(x_ref[...], g_ref[...]).astype(o_ref.dtype)


def _norm_cast(x, g, *, tm=1024):
    M, D = x.shape
    return pl.pallas_call(
        _norm_body,
        grid=(M // tm,),
        in_specs=[pl.BlockSpec((tm, D), lambda i: (i, 0)), pl.BlockSpec((1, D), lambda i: (0, 0))],
        out_specs=pl.BlockSpec((tm, D), lambda i: (i, 0)),
        out_shape=jax.ShapeDtypeStruct((M, D), BF16),
        compiler_params=_params(("parallel",)),
        name="norm_mix",
    )(x, g.reshape(1, D).astype(F32))


def _mm_body(*refs, n_extra, epi, split):
    a_ref, w_ref = refs[0], refs[1]
    extras = refs[2:2 + n_extra]
    o_ref = refs[2 + n_extra]
    acc = jnp.dot(a_ref[...], w_ref[...], preferred_element_type=F32)
    res = epi(acc, *[e[...] for e in extras])
    if split:
        for g in range(split):
            o_ref[g] = res[:, g * HEAD_DIM:(g + 1) * HEAD_DIM].astype(o_ref.dtype)
    else:
        o_ref[...] = res.astype(o_ref.dtype)


def _mm(a, w, epi, extras=(), *, out_dtype, tm=1024, tn=512, split=0, name):
    M, K = a.shape
    N = w.shape[1]
    tn = min(tn, N)
    grid = (M // tm, N // tn)
    in_specs = [pl.BlockSpec((tm, K), lambda i, j: (i, 0)),
                pl.BlockSpec((K, tn), lambda i, j: (0, j))]
    args = [a, w]
    for arr, kind, off in extras:
        if kind == 'row':
            in_specs.append(pl.BlockSpec((1, tn), lambda i, j, off=off: (0, j + off)))
        else:
            in_specs.append(pl.BlockSpec((tm, tn), lambda i, j, off=off: (i, j + off)))
        args.append(arr)
    if split:
        out_shape = jax.ShapeDtypeStruct((N // tn, split, M, HEAD_DIM), out_dtype)
        out_spec = pl.BlockSpec((None, split, tm, HEAD_DIM), lambda i, j: (j, 0, i, 0))
    else:
        out_shape = jax.ShapeDtypeStruct((M, N), out_dtype)
        out_spec = pl.BlockSpec((tm, tn), lambda i, j: (i, j))
    return pl.pallas_call(
        functools.partial(_mm_body, n_extra=len(extras), epi=epi, split=split),
        grid=grid, in_specs=in_specs, out_specs=out_spec, out_shape=out_shape,
        compiler_params=_params(("parallel", "arbitrary")),
        name=name,
    )(*args)


def _merge_body(a_ref, b_ref, wa_ref, wb_ref, ga_ref, gb_ref, o_ref):
    ya = jnp.dot(a_ref[...], wa_ref[...], preferred_element_type=F32)
    yb = jnp.dot(b_ref[...], wb_ref[...], preferred_element_type=F32)
    o_ref[...] = (ga_ref[...].astype(F32) * ya + gb_ref[...].astype(F32) * yb).astype(o_ref.dtype)


def _merge(a, b, wa, wb, gates, *, tm=1024, tn=512):
    M, K = a.shape
    N = wa.shape[1]
    nt = N // tn
    lhs = pl.BlockSpec((tm, K), lambda i, j: (i, 0))
    wsp = pl.BlockSpec((K, tn), lambda i, j: (0, j))
    return pl.pallas_call(
        _merge_body,
        grid=(M // tm, nt),
        in_specs=[lhs, lhs, wsp, wsp, pl.BlockSpec((tm, tn), lambda i, j: (i, j)),
                  pl.BlockSpec((tm, tn), lambda i, j: (i, j + nt))],
        out_specs=pl.BlockSpec((tm, tn), lambda i, j: (i, j)),
        out_shape=jax.ShapeDtypeStruct((M, N), BF16),
        compiler_params=_params(("parallel", "arbitrary")),
        name="mixer_merge",
    )(a, b, wa, wb, gates, gates)


def _lru_body(u_ref, gy_ref, cw_ref, cb_ref, wa_ref, ba_ref, wx_ref, bx_ref, lam_ref,
              o_ref, xe_ref, xc_ref, a_ref, b_ref, hn_ref, h_ref, *, T, ct):
    L = T // 8
    W = LRU_BLOCK_W
    nb = ct // W
    s = pl.program_id(2)

    @pl.when(s == 0)
    def _():
        xe_ref[:, 0:8, :] = jnp.zeros((nb, 8, W), F32)
        h_ref[...] = jnp.zeros((nb, 8, W), F32)

    lam = lam_ref[...]
    neg_softplus = -(jnp.maximum(-lam, 0.0) + jnp.log1p(jnp.exp(-jnp.abs(lam))))
    log_a_scale = LRU_C * neg_softplus
    for n in range(nb):
        sl = slice(n * W, (n + 1) * W)
        xe_ref[n, 8:, :] = u_ref[:, sl]

        def seg_rows(off):
            return xe_ref[n, pl.ds(off, 8, stride=L), :]

        taps = [jnp.broadcast_to(cw_ref[k:k + 1, sl], (8, W)) for k in range(CONV_WIDTH)]
        bias = jnp.broadcast_to(cb_ref[:, sl], (8, W))
        prev = [seg_rows(8 - CONV_WIDTH + 1 + k) for k in range(CONV_WIDTH - 1)]
        for t in range(L):
            win = prev + [seg_rows(8 + t)]
            y = bias
            for k in range(CONV_WIDTH):
                y = y + taps[k] * win[k]
            xc_ref[t * 8:(t + 1) * 8, :] = y
            prev = win[1:]
        xe_ref[n, 0:8, :] = xe_ref[n, T:T + 8, :]

        xc = xc_ref[...]
        xb = xc.astype(BF16)
        r = jax.nn.sigmoid(jnp.dot(xb, wa_ref[n], preferred_element_type=F32) + ba_ref[:, sl])
        i = jax.nn.sigmoid(jnp.dot(xb, wx_ref[n], preferred_element_type=F32) + bx_ref[:, sl])
        a = jnp.exp(r * log_a_scale[:, sl])
        a_ref[n] = a
        b_ref[n] = jnp.sqrt(jnp.maximum(1.0 - a * a, 0.0)) * (i * xc)

    def step(t, carry):
        rows = pl.ds(pl.multiple_of(t * 8, 8), 8)
        out = []
        for n in range(nb):
            h, decay = carry[n]
            a_t = a_ref[n, rows, :]
            h = a_t * h + b_ref[n, rows, :]
            decay = a_t * decay
            b_ref[n, rows, :] = h
            a_ref[n, rows, :] = decay
            out.append((h, decay))
        return tuple(out)

    init = tuple((jnp.zeros((8, W), F32), jnp.ones((8, W), F32)) for _ in range(nb))
    ends = lax.fori_loop(0, L, step, init, unroll=8)
    row = lax.broadcasted_iota(jnp.int32, (8, W), 0)
    for n in range(nb):
        h_end, a_end = ends[n]
        h_in = h_ref[n]
        for r in range(1, 8):
            h_in = jnp.where(row == r, pltpu.roll(a_end * h_in + h_end, 1, axis=0), h_in)
        h_ref[n] = jnp.broadcast_to((a_end * h_in + h_end)[7:8, :], (8, W))
        for t in range(L):
            rows = slice(t * 8, (t + 1) * 8)
            hn_ref[pl.ds(t, 8, stride=L), :] = b_ref[n, rows, :] + a_ref[n, rows, :] * h_in
        sl = slice(n * W, (n + 1) * W)
        o_ref[:, sl] = (hn_ref[...] * gy_ref[:, sl].astype(F32)).astype(o_ref.dtype)


def _rg_lru(u_lx, gelu_y, conv_w, conv_b, w_a, b_a, w_x, b_x, lam, *, B, S, T=512, ct=512):
    N, C = u_lx.shape
    nb = ct // LRU_BLOCK_W
    ns = S // T
    row = lambda v: v.reshape(1, C).astype(F32)
    tile = pl.BlockSpec((T, ct), lambda b, c, s: (b * ns + s, c))
    vec = pl.BlockSpec((1, ct), lambda b, c, s: (0, c))
    wblk = pl.BlockSpec((nb, LRU_BLOCK_W, LRU_BLOCK_W), lambda b, c, s: (c, 0, 0))
    return pl.pallas_call(
        functools.partial(_lru_body, T=T, ct=ct),
        grid=(B, C // ct, ns),
        in_specs=[tile, tile, pl.BlockSpec((CONV_WIDTH, ct), lambda b, c, s: (0, c)), vec,
                  wblk, vec, wblk, vec, vec],
        out_specs=tile,
        out_shape=jax.ShapeDtypeStruct((N, C), BF16),
        scratch_shapes=[pltpu.VMEM((nb, T + 8, LRU_BLOCK_W), F32), pltpu.VMEM((T, LRU_BLOCK_W), F32),
                        pltpu.VMEM((nb, T, LRU_BLOCK_W), F32), pltpu.VMEM((nb, T, LRU_BLOCK_W), F32),
                        pltpu.VMEM((T, LRU_BLOCK_W), F32), pltpu.VMEM((nb, 8, LRU_BLOCK_W), F32)],
        compiler_params=_params(("parallel", "parallel", "arbitrary")),
        name="rg_lru",
    )(u_lx, gelu_y, conv_w.astype(F32), row(conv_b), w_a.astype(BF16), row(b_a),
      w_x.astype(BF16), row(b_x), row(lam))


def _cmp_body(x_ref, pos_ref, w1_ref, w2_ref, o_ref, *, nc, half):
    x = x_ref[...]
    first = jnp.dot(x, w1_ref[0:half, :], preferred_element_type=F32)
    second = jnp.dot(x, w1_ref[half:, :], preferred_element_type=F32)
    pos = jnp.broadcast_to(pos_ref[...], (8, 2 * half)).astype(BF16)
    cpos = jnp.dot(pos, w1_ref[...], preferred_element_type=F32)[0:1, :]
    hid = first + pltpu.roll(second, nc - 1, axis=0) + cpos
    o_ref[...] = jnp.dot(_gelu(hid).astype(BF16), w2_ref[...],
                         preferred_element_type=F32).astype(o_ref.dtype)


def _compress(x, pos, w1, w2, *, B, name):
    G, rows, half = x.shape
    nc = rows // B
    hidden = w1.shape[1]
    return pl.pallas_call(
        functools.partial(_cmp_body, nc=nc, half=half),
        grid=(B, G),
        in_specs=[pl.BlockSpec((None, nc, half), lambda b, g: (g, b, 0)),
                  pl.BlockSpec((1, 2 * half), lambda b, g: (0, 0)),
                  pl.BlockSpec((2 * half, hidden), lambda b, g: (0, 0)),
                  pl.BlockSpec((hidden, HEAD_DIM), lambda b, g: (0, 0))],
        out_specs=pl.BlockSpec((None, None, nc, HEAD_DIM), lambda b, g: (b, g, 0, 0)),
        out_shape=jax.ShapeDtypeStruct((B, G, nc, HEAD_DIM), BF16),
        compiler_params=_params(("parallel", "parallel")),
        name=name,
    )(x, pos.reshape(1, 2 * half).astype(F32), w1.astype(BF16), w2.astype(BF16))


def _qk(q, k):
    return lax.dot_general(q, k, (((1,), (1,)), ((), ())), preferred_element_type=F32)


def _nsa_body(q_ref, kc_ref, vc_ref, ks_ref, vs_ref, kw_ref, vw_ref, g_ref, o_ref,
              kaug_ref, vaug_ref, vwaug_ref, ovt_ref, qaug_ref, m_ref, acc_ref,
              sbuf_ref, part_ref, *, S, nc, n_sb, n_sel, tk):
    R = HPG * Q_TILE
    D2 = 2 * HEAD_DIM
    i = pl.program_id(2)
    s0 = i * Q_TILE

    @pl.when(i == 0)
    def _():
        key_blk = lax.broadcasted_iota(jnp.int32, (S, NB), 0) // SLC_BLOCK
        col = lax.broadcasted_iota(jnp.int32, (S, NB), 1)
        kaug_ref[:, 0:HEAD_DIM] = ks_ref[...]
        kaug_ref[:, HEAD_DIM:D2] = (key_blk == col).astype(BF16)
        vaug_ref[:, 0:HEAD_DIM] = vs_ref[...]
        vaug_ref[:, HEAD_DIM:D2] = jnp.ones((S, HEAD_DIM), BF16)
        vwaug_ref[:, 0:HEAD_DIM] = vw_ref[...]
        vwaug_ref[:, HEAD_DIM:D2] = jnp.ones((S, HEAD_DIM), BF16)
        ci = lax.broadcasted_iota(jnp.int32, (NB, nc), 1) * CMP_STRIDE
        si = lax.broadcasted_iota(jnp.int32, (NB, nc), 0) * SLC_BLOCK
        ovt_ref[...] = ((ci < si + SLC_BLOCK) & (si < ci + CMP_BLOCK)).astype(BF16)

    qb = q_ref[...]
    q4 = jnp.concatenate([qb[:, h * HEAD_DIM:(h + 1) * HEAD_DIM] for h in range(HPG)], axis=0)
    tq = s0 + lax.broadcasted_iota(jnp.int32, (Q_TILE, 1), 0)

    def softmax_pv(s, bias, v_aug):
        k = s.shape[-1]
        s3 = s.reshape(HPG, Q_TILE, k) + bias[None]
        e = jnp.exp2(s3 - jnp.max(s3, axis=-1, keepdims=True))
        return e, jnp.dot(e.reshape(R, k).astype(BF16), v_aug, preferred_element_type=F32)

    cend = lax.broadcasted_iota(jnp.int32, (1, nc), 1) * CMP_STRIDE + (CMP_BLOCK - 1)
    e_c, oc = softmax_pv(_qk(q4, kc_ref[...]), jnp.where(cend <= tq, 0.0, NEG), vc_ref[...])
    inv_c = 1.0 / jnp.maximum(jnp.sum(e_c, axis=-1, keepdims=True), 1e-30)
    any_c = (tq >= CMP_BLOCK - 1).astype(F32)
    o_cmp = oc.reshape(HPG, Q_TILE, HEAD_DIM) * (inv_c * any_c[None])

    span = WINDOW + Q_TILE
    w0 = pl.multiple_of(jnp.maximum(s0 - WINDOW, 0), Q_TILE)
    diff = tq - (w0 + lax.broadcasted_iota(jnp.int32, (1, span), 1))
    _, ow = softmax_pv(_qk(q4, kw_ref[pl.ds(w0, span), :]),
                       jnp.where((diff >= 0) & (diff < WINDOW), 0.0, NEG),
                       vwaug_ref[pl.ds(w0, span), :])
    o_win = (ow[:, :HEAD_DIM] / jnp.maximum(ow[:, HEAD_DIM:], 1e-30)
             ).reshape(HPG, Q_TILE, HEAD_DIM)
    gt = g_ref[...]
    for h in range(HPG):
        part_ref[:, h * HEAD_DIM:(h + 1) * HEAD_DIM] = (
            gt[:, h:h + 1] * o_cmp[h] + gt[:, 2 * HPG + h:2 * HPG + h + 1] * o_win[h])

    pc_sum = e_c[0] * inv_c[0]
    for h in range(1, HPG):
        pc_sum = pc_sum + e_c[h] * inv_c[h]
    pc_hi = pc_sum.astype(BF16)
    pc_lo = (pc_sum - pc_hi.astype(F32)).astype(BF16)
    ovt = ovt_ref[...]
    imp = _qk(ovt, pc_hi) + _qk(ovt, pc_lo)
    blk = lax.broadcasted_iota(jnp.int32, (NB, 1), 0)
    cur = (s0 + lax.broadcasted_iota(jnp.int32, (1, Q_TILE), 1)) // SLC_BLOCK
    forced = (blk == 0) | (blk == cur) | (blk == cur - 1)
    score = jnp.where(forced, KNOCKED_OUT, jnp.where(blk <= cur, imp, -FORCED_SCORE))
    if n_sb < NB:
        score = jnp.where(blk < n_sb, score, KNOCKED_OUT)
    blk_f = blk.astype(F32)
    sel = forced.astype(F32)
    for _ in range(n_sel - 3):
        mx = jnp.max(score, axis=0, keepdims=True)
        first = jnp.min(jnp.where(score == mx, blk_f, float(NB)), axis=0, keepdims=True)
        hit = blk_f == first
        sel = jnp.where(hit, 1.0, sel)
        score = jnp.where(hit, KNOCKED_OUT, score)
    def mask_columns(sel_nq):
        bias = ((sel_nq - 1.0) * (-NEG)).T.astype(BF16)
        return jnp.concatenate([bias] * HPG, axis=0)

    qaug_ref[:, 0:HEAD_DIM] = q4
    qaug_ref[:, HEAD_DIM:D2] = mask_columns(jnp.where(blk < s0 // SLC_BLOCK, sel, 0.0))

    own = pl.ds(pl.multiple_of(s0, Q_TILE), Q_TILE)
    lower = (lax.broadcasted_iota(jnp.int32, (Q_TILE, Q_TILE), 1)
             <= lax.broadcasted_iota(jnp.int32, (Q_TILE, Q_TILE), 0))
    q_own = jnp.concatenate([q4, mask_columns(sel)], axis=1)
    s_own = (_qk(q_own, kaug_ref[own, :]).reshape(HPG, Q_TILE, Q_TILE)
             + jnp.where(lower, 0.0, NEG)[None]).reshape(R, Q_TILE)
    m_own = jnp.broadcast_to(jnp.max(s_own, axis=-1, keepdims=True), (R, HEAD_DIM))
    m_ref[...] = m_own
    p_own = jnp.exp2(s_own - jnp.concatenate([m_own] * (Q_TILE // HEAD_DIM), axis=1))
    acc_ref[...] = jnp.dot(p_own.astype(BF16), vaug_ref[own, :],
                           preferred_element_type=F32)
    n_past = (s0 + tk - 1) // tk

    def scores(kt):
        return _qk(qaug_ref[...], kaug_ref[pl.ds(pl.multiple_of(kt * tk, tk), tk), :])

    def consume(kt):
        s = sbuf_ref[...]
        m_old = m_ref[...]
        m_new = jnp.maximum(m_old, jnp.max(s, axis=-1, keepdims=True))
        p = jnp.exp2(s - jnp.concatenate([m_new] * (tk // HEAD_DIM), axis=1)).astype(BF16)
        alpha = jnp.exp2(m_old - m_new)
        m_ref[...] = m_new
        acc_ref[...] = (jnp.concatenate([alpha, alpha], axis=1) * acc_ref[...]
                        + jnp.dot(p, vaug_ref[pl.ds(pl.multiple_of(kt * tk, tk), tk), :],
                                  preferred_element_type=F32))

    sbuf_ref[...] = scores(0)

    @pl.loop(0, jnp.maximum(n_past - 1, 0))
    def _(kt):
        s_next = scores(kt + 1)
        consume(kt)
        sbuf_ref[...] = s_next

    consume(jnp.maximum(n_past - 1, 0))
    acc_s = acc_ref[...]
    o_slc = (acc_s[:, :HEAD_DIM] / jnp.maximum(acc_s[:, HEAD_DIM:], 1e-30)
             ).reshape(HPG, Q_TILE, HEAD_DIM)

    gt = g_ref[...]
    for h in range(HPG):
        cols = slice(h * HEAD_DIM, (h + 1) * HEAD_DIM)
        o_ref[:, cols] = (part_ref[:, cols]
                          + gt[:, HPG + h:HPG + h + 1] * o_slc[h]).astype(o_ref.dtype)


def _nsa(q, kc, vc, kv6, gates, *, B, S, tk=1024):
    N = B * S
    nq = S // Q_TILE
    nc = kc.shape[2]
    n_sb = S // SLC_BLOCK
    assert n_sb <= NB and nc % HEAD_DIM == 0 and S % tk == 0
    G = N_KV_GROUPS
    gw = HPG * HEAD_DIM
    kv_spec = lambda t: pl.BlockSpec((None, None, S, HEAD_DIM), lambda b, g, i, t=t: (t, g, b, 0))
    cmp_spec = pl.BlockSpec((None, None, nc, HEAD_DIM), lambda b, g, i: (b, g, 0, 0))
    return pl.pallas_call(
        functools.partial(_nsa_body, S=S, nc=nc, n_sb=n_sb, n_sel=min(N_SELECT, n_sb), tk=tk),
        grid=(B, G, nq),
        in_specs=[pl.BlockSpec((Q_TILE, gw), lambda b, g, i: (b * nq + i, g)),
                  cmp_spec, cmp_spec, kv_spec(2), kv_spec(3), kv_spec(4), kv_spec(5),
                  pl.BlockSpec((Q_TILE, 128), lambda b, g, i: (b * nq + i, g))],
        out_specs=pl.BlockSpec((Q_TILE, gw), lambda b, g, i: (b * nq + i, g)),
        out_shape=jax.ShapeDtypeStruct((N, G * gw), BF16),
        scratch_shapes=[pltpu.VMEM((S, 2 * HEAD_DIM), BF16), pltpu.VMEM((S, 2 * HEAD_DIM), BF16),
                        pltpu.VMEM((S, 2 * HEAD_DIM), BF16),
                        pltpu.VMEM((NB, nc), BF16),
                        pltpu.VMEM((HPG * Q_TILE, 2 * HEAD_DIM), BF16),
                        pltpu.VMEM((HPG * Q_TILE, HEAD_DIM), F32),
                        pltpu.VMEM((HPG * Q_TILE, 2 * HEAD_DIM), F32),
                        pltpu.VMEM((HPG * Q_TILE, tk), F32),
                        pltpu.VMEM((Q_TILE, HPG * HEAD_DIM), F32)],
        compiler_params=_params(("parallel", "parallel", "arbitrary")),
        name="nsa_attention",
    )(q, kc, vc, kv6, kv6, kv6, kv6, gates)


def _mlp_body(x_ref, g_ref, wu_ref, wd_ref, o_ref, h_ref, acc_ref):
    f = pl.program_id(1)

    @pl.when(f == 0)
    def _():
        h_ref[...] = _rms(x_ref[...], g_ref[...]).astype(BF16)
        acc_ref[...] = jnp.zeros_like(acc_ref)

    up = jnp.dot(h_ref[...], wu_ref[...], preferred_element_type=F32)
    up = jnp.square(jnp.maximum(up, 0.0)).astype(BF16)
    acc_ref[...] += jnp.dot(up, wd_ref[...], preferred_element_type=F32)

    @pl.when(f == pl.num_programs(1) - 1)
    def _():
        o_ref[...] = x_ref[...] + acc_ref[...]


def _mlp(x, g, w_up, w_down, *, tm=512, tf=512):
    M, D = x.shape
    FF = w_up.shape[1]
    return pl.pallas_call(
        _mlp_body,
        grid=(M // tm, FF // tf),
        in_specs=[pl.BlockSpec((tm, D), lambda i, f: (i, 0)),
                  pl.BlockSpec((1, D), lambda i, f: (0, 0)),
                  pl.BlockSpec((D, tf), lambda i, f: (0, f)),
                  pl.BlockSpec((tf, D), lambda i, f: (f, 0))],
        out_specs=pl.BlockSpec((tm, D), lambda i, f: (i, 0)),
        out_shape=jax.ShapeDtypeStruct((M, D), F32),
        scratch_shapes=[pltpu.VMEM((tm, D), BF16), pltpu.VMEM((tm, D), F32)],
        compiler_params=_params(("parallel", "arbitrary")),
        name="mlp",
    )(x, g.reshape(1, D).astype(F32), w_up, w_down)


def _ple_body(x_ref, p_ref, g_ref, wg_ref, wp_ref, gf_ref, o_ref, *, final_norm):
    x = x_ref[...]
    h = _rms(x, g_ref[...]).astype(BF16)
    gate = jax.nn.sigmoid(jnp.dot(h, wg_ref[...], preferred_element_type=F32))
    proj = jnp.dot(p_ref[...].astype(BF16), wp_ref[...], preferred_element_type=F32)
    y = x + gate * proj
    o_ref[...] = _rms(y, gf_ref[...]) if final_norm else y


def _ple(x, p, g, w_gate, w_proj, g_final, *, final_norm, tm=512):
    M, D = x.shape
    P = p.shape[1]
    row = lambda v: v.reshape(1, D).astype(F32)
    vec = pl.BlockSpec((1, D), lambda i: (0, 0))
    return pl.pallas_call(
        functools.partial(_ple_body, final_norm=final_norm),
        grid=(M // tm,),
        in_specs=[pl.BlockSpec((tm, D), lambda i: (i, 0)), pl.BlockSpec((tm, P), lambda i: (i, 0)),
                  vec, pl.BlockSpec((D, D), lambda i: (0, 0)), pl.BlockSpec((P, D), lambda i: (0, 0)),
                  vec],
        out_specs=pl.BlockSpec((tm, D), lambda i: (i, 0)),
        out_shape=jax.ShapeDtypeStruct((M, D), F32),
        compiler_params=_params(("parallel",)),
        name="ple_final",
    )(x, p, row(g), w_gate, w_proj, row(g_final))


def _layer(x, p, norm_mix_g, w_in, b_in, conv_w, conv_b, w_gate_a, b_gate_a, w_gate_x, b_gate_x,
           lru_lambda, w_lru_out, cmp_pos_k, cmp_w1_k, cmp_w2_k, cmp_pos_v, cmp_w1_v, cmp_w2_v,
           w_attn_out, w_o, norm_mlp_g, w_up, w_down, norm_ple_g, w_ple_gate, w_ple_proj,
           norm_final_g, *, B, S, final_norm):
    N, D = x.shape
    C = conv_w.shape[1]
    G = N_KV_GROUPS
    QW = N_HEADS * HEAD_DIM
    KVW = G * HEAD_DIM
    o_ly, o_q, o_kv, o_g, o_m = C, 2 * C, 2 * C + QW, 2 * C + QW + 6 * KVW, 2 * C + QW + 6 * KVW + 3 * N_HEADS
    wb = lambda lo, hi: (w_in[:, lo:hi].astype(BF16), b_in[lo:hi].reshape(1, hi - lo).astype(F32))
    proj = functools.partial(_mm, _norm_cast(x, norm_mix_g))

    w, b = wb(0, o_ly)
    u_lx = proj(w, lambda acc, b: acc + b, [(b, 'row', 0)], out_dtype=F32, name="in_lru_x")
    w, b = wb(o_ly, o_q)
    gelu_y = proj(w, lambda acc, b: _gelu(acc + b), [(b, 'row', 0)], out_dtype=BF16, name="in_lru_y")
    w, b = wb(o_q, o_kv)
    q = proj(w, lambda acc, b: (acc + b) * (HEAD_DIM ** -0.5 * LOG2E), [(b, 'row', 0)],
             out_dtype=BF16, name="in_q")
    w, b = wb(o_kv, o_g)
    kv6 = proj(w, lambda acc, b: acc + b, [(b, 'row', 0)], out_dtype=BF16, tn=KVW, split=G,
               name="in_kv")
    wg = w_in[:, o_g:o_m].reshape(D, 3, G, HPG).transpose(0, 2, 1, 3).reshape(D, G, 3 * HPG)
    wg = jnp.pad(wg, ((0, 0), (0, 0), (0, 128 - 3 * HPG))).reshape(D, G * 128).astype(BF16)
    bg = b_in[o_g:o_m].reshape(3, G, HPG).transpose(1, 0, 2).reshape(G, 3 * HPG)
    bg = jnp.pad(bg, ((0, 0), (0, 128 - 3 * HPG))).reshape(1, G * 128).astype(F32)
    gates = proj(wg, lambda acc, b: jax.nn.sigmoid(acc + b), [(bg, 'row', 0)], out_dtype=F32,
                 name="in_gates")
    w, b = wb(o_m, o_m + 2 * D)
    gm = proj(w, lambda acc, b: jax.nn.sigmoid(acc + b), [(b, 'row', 0)], out_dtype=BF16,
              name="in_merge_gates")

    a_in = _rg_lru(u_lx, gelu_y, conv_w, conv_b, w_gate_a, b_gate_a, w_gate_x, b_gate_x,
                   lru_lambda, B=B, S=S)
    nc = S // CMP_STRIDE
    kc = _compress(kv6[0].reshape(G, B * nc, CMP_STRIDE * HEAD_DIM), cmp_pos_k, cmp_w1_k, cmp_w2_k,
                   B=B, name="compress_k")
    vc = _compress(kv6[1].reshape(G, B * nc, CMP_STRIDE * HEAD_DIM), cmp_pos_v, cmp_w1_v, cmp_w2_v,
                   B=B, name="compress_v")
    o_nsa = _nsa(q, kc, vc, kv6, gates, B=B, S=S)

    mixed = _merge(a_in, o_nsa, w_lru_out.astype(BF16), w_attn_out.astype(BF16), gm)
    x1 = _mm(mixed, w_o.astype(BF16), lambda acc, xr: xr + acc, [(x, 'tile', 0)],
             out_dtype=F32, name="out_proj")
    x2 = _mlp(x1, norm_mlp_g, w_up.astype(BF16), w_down.astype(BF16))
    return _ple(x2, p, norm_ple_g, w_ple_gate.astype(BF16), w_ple_proj.astype(BF16),
                norm_final_g, final_norm=final_norm)


def kernel(x, p, norm_mix_g, w_in, b_in, conv_w, conv_b, w_gate_a, b_gate_a, w_gate_x, b_gate_x, lru_lambda, w_lru_out, cmp_pos_k, cmp_w1_k, cmp_w2_k, cmp_pos_v, cmp_w1_v, cmp_w2_v, w_attn_out, w_o, norm_mlp_g, w_up, w_down, norm_ple_g, w_ple_gate, w_ple_proj, norm_final_g):
    B, S, D = x.shape
    depth = w_in.shape[0]
    assert S % 512 == 0 and S >= WINDOW + Q_TILE and D % 512 == 0
    xs = x.reshape(B * S, D)
    for i in range(depth):
        per_layer = [t[i] for t in (
            norm_mix_g, w_in, b_in, conv_w, conv_b, w_gate_a, b_gate_a, w_gate_x, b_gate_x,
            lru_lambda, w_lru_out, cmp_pos_k, cmp_w1_k, cmp_w2_k, cmp_pos_v, cmp_w1_v, cmp_w2_v,
            w_attn_out, w_o, norm_mlp_g, w_up, w_down, norm_ple_g, w_ple_gate, w_ple_proj)]
        xs = _layer(xs, p[i].reshape(B * S, -1), *per_layer, norm_final_g, B=B, S=S,
                    final_norm=i == depth - 1)
    return xs.reshape(B, S, D)
```

```python
import functools

import jax
import jax.numpy as jnp
from jax import lax
from jax.experimental import pallas as pl
from jax.experimental.pallas import tpu as pltpu

F32 = jnp.float32
BF16 = jnp.bfloat16

LRU_BLOCK_W = 128
CONV_WIDTH = 4
LRU_C = 8.0
N_HEADS = 16
N_KV_GROUPS = 4
HPG = N_HEADS // N_KV_GROUPS
HEAD_DIM = 128
CMP_STRIDE = 16
CMP_BLOCK = 32
SLC_BLOCK = 64
N_SELECT = 16
WINDOW = 512
Q_TILE = 256
EPS = 1e-6
NEG = -1e30
FORCED_SCORE = 1e4
KNOCKED_OUT = -3e38
NB = 128
LOG2E = 1.4426950408889634

VMEM_LIMIT = 56 * 1024 * 1024


def _params(sem):
    return pltpu.CompilerParams(dimension_semantics=sem, vmem_limit_bytes=VMEM_LIMIT)


def _rms(xf, g):
    return xf * lax.rsqrt(jnp.mean(xf * xf, axis=-1, keepdims=True) + EPS) * g


def _gelu(x):
    return jax.nn.gelu(x)


def _norm_body(x_ref, g_ref, o_ref):
    o_ref[...] = _rms(x_ref[...], g_ref[...]).astype(o_ref.dtype)


def _norm_cast(x, g, *, tm=1024):
    M, D = x.shape
    return pl.pallas_call(
        _norm_body,
        grid=(M // tm,),
        in_specs=[pl.BlockSpec((tm, D), lambda i: (i, 0)), pl.BlockSpec((1, D), lambda i: (0, 0))],
        out_specs=pl.BlockSpec((tm, D), lambda i: (i, 0)),
        out_shape=jax.ShapeDtypeStruct((M, D), BF16),
        compiler_params=_params(("parallel",)),
        name="norm_mix",
    )(x, g.reshape(1, D).astype(F32))


def _mm_body(*refs, n_extra, epi, split, pack):
    a_ref, w_ref = refs[0], refs[1]
    extras = refs[2:2 + n_extra]
    o_ref = refs[2 + n_extra]
    acc = jnp.dot(a_ref[...], w_ref[...], preferred_element_type=F32)
    res = epi(acc, *[e[...] for e in extras])
    if split and pack:
        stage_ref = refs[3 + n_extra]
        rows = res.shape[0] // pack
        for g in range(split):
            stage_ref[...] = res[:, g * HEAD_DIM:(g + 1) * HEAD_DIM]
            for t in range(pack):
                o_ref[g, :, t * HEAD_DIM:(t + 1) * HEAD_DIM] = (
                    stage_ref[pl.ds(t, rows, stride=pack), :].astype(o_ref.dtype))
    elif split:
        for g in range(split):
            o_ref[g] = res[:, g * HEAD_DIM:(g + 1) * HEAD_DIM].astype(o_ref.dtype)
    else:
        o_ref[...] = res.astype(o_ref.dtype)


def _mm(a, w, epi, extras=(), *, out_dtype, tm=1024, tn=1024, split=0, pack=0, name):
    M, K = a.shape
    N = w.shape[1]
    tn = min(tn, N)
    grid = (M // tm, N // tn)
    in_specs = [pl.BlockSpec((tm, K), lambda i, j: (i, 0)),
                pl.BlockSpec((K, tn), lambda i, j: (0, j))]
    args = [a, w]
    for arr, kind, off in extras:
        if kind == 'row':
            in_specs.append(pl.BlockSpec((1, tn), lambda i, j, off=off: (0, j + off)))
        else:
            in_specs.append(pl.BlockSpec((tm, tn), lambda i, j, off=off: (i, j + off)))
        args.append(arr)
    scratch = []
    if split and pack:
        out_shape = jax.ShapeDtypeStruct((N // tn, split, M // pack, pack * HEAD_DIM), out_dtype)
        out_spec = pl.BlockSpec((None, split, tm // pack, pack * HEAD_DIM),
                                lambda i, j: (j, 0, i, 0))
        scratch = [pltpu.VMEM((tm, HEAD_DIM), F32)]
    elif split:
        out_shape = jax.ShapeDtypeStruct((N // tn, split, M, HEAD_DIM), out_dtype)
        out_spec = pl.BlockSpec((None, split, tm, HEAD_DIM), lambda i, j: (j, 0, i, 0))
    else:
        out_shape = jax.ShapeDtypeStruct((M, N), out_dtype)
        out_spec = pl.BlockSpec((tm, tn), lambda i, j: (i, j))
    return pl.pallas_call(
        functools.partial(_mm_body, n_extra=len(extras), epi=epi, split=split, pack=pack),
        grid=grid, in_specs=in_specs, out_specs=out_spec, out_shape=out_shape,
        scratch_shapes=scratch,
        compiler_params=_params(("parallel", "arbitrary")),
        name=name,
    )(*args)


def _merge_body(a_ref, b_ref, wa_ref, wb_ref, ga_ref, gb_ref, o_ref):
    ya = jnp.dot(a_ref[...], wa_ref[...], preferred_element_type=F32)
    yb = jnp.dot(b_ref[...], wb_ref[...], preferred_element_type=F32)
    o_ref[...] = (ga_ref[...].astype(F32) * ya + gb_ref[...].astype(F32) * yb).astype(o_ref.dtype)


def _merge(a, b, wa, wb, gates, *, tm=1024, tn=512):
    M, K = a.shape
    N = wa.shape[1]
    nt = N // tn
    lhs = pl.BlockSpec((tm, K), lambda i, j: (i, 0))
    wsp = pl.BlockSpec((K, tn), lambda i, j: (0, j))
    return pl.pallas_call(
        _merge_body,
        grid=(M // tm, nt),
        in_specs=[lhs, lhs, wsp, wsp, pl.BlockSpec((tm, tn), lambda i, j: (i, j)),
                  pl.BlockSpec((tm, tn), lambda i, j: (i, j + nt))],
        out_specs=pl.BlockSpec((tm, tn), lambda i, j: (i, j)),
        out_shape=jax.ShapeDtypeStruct((M, N), BF16),
        compiler_params=_params(("parallel", "arbitrary")),
        name="mixer_merge",
    )(a, b, wa, wb, gates, gates)


def _lru_body(u_ref, gy_ref, cw_ref, cb_ref, wa_ref, ba_ref, wx_ref, bx_ref, lam_ref,
              o_ref, xe_ref, xc_ref, a_ref, b_ref, hn_ref, h_ref, *, T, ct):
    L = T // 8
    W = LRU_BLOCK_W
    nb = ct // W
    s = pl.program_id(2)

    @pl.when(s == 0)
    def _():
        xe_ref[:, 0:8, :] = jnp.zeros((nb, 8, W), F32)
        h_ref[...] = jnp.zeros((nb, 8, W), F32)

    lam = lam_ref[...]
    neg_softplus = -(jnp.maximum(-lam, 0.0) + jnp.log1p(jnp.exp(-jnp.abs(lam))))
    log_a_scale = LRU_C * neg_softplus
    for n in range(nb):
        sl = slice(n * W, (n + 1) * W)
        xe_ref[n, 8:, :] = u_ref[:, sl]

        def seg_rows(off):
            return xe_ref[n, pl.ds(off, 8, stride=L), :]

        taps = [jnp.broadcast_to(cw_ref[k:k + 1, sl], (8, W)) for k in range(CONV_WIDTH)]
        bias = jnp.broadcast_to(cb_ref[:, sl], (8, W))
        prev = [seg_rows(8 - CONV_WIDTH + 1 + k) for k in range(CONV_WIDTH - 1)]
        for t in range(L):
            win = prev + [seg_rows(8 + t)]
            y = bias
            for k in range(CONV_WIDTH):
                y = y + taps[k] * win[k]
            xc_ref[t * 8:(t + 1) * 8, :] = y
            prev = win[1:]
        xe_ref[n, 0:8, :] = xe_ref[n, T:T + 8, :]

        xc = xc_ref[...]
        xb = xc.astype(BF16)
        r = jax.nn.sigmoid(jnp.dot(xb, wa_ref[n], preferred_element_type=F32) + ba_ref[:, sl])
        i = jax.nn.sigmoid(jnp.dot(xb, wx_ref[n], preferred_element_type=F32) + bx_ref[:, sl])
        a = jnp.exp(r * log_a_scale[:, sl])
        a_ref[n] = a
        b_ref[n] = jnp.sqrt(jnp.maximum(1.0 - a * a, 0.0)) * (i * xc)

    def step(t, carry):
        rows = pl.ds(pl.multiple_of(t * 8, 8), 8)
        out = []
        for n in range(nb):
            h, decay = carry[n]
            a_t = a_ref[n, rows, :]
            h = a_t * h + b_ref[n, rows, :]
            decay = a_t * decay
            b_ref[n, rows, :] = h
            a_ref[n, rows, :] = decay
            out.append((h, decay))
        return tuple(out)

    init = tuple((jnp.zeros((8, W), F32), jnp.ones((8, W), F32)) for _ in range(nb))
    ends = lax.fori_loop(0, L, step, init, unroll=8)
    row = lax.broadcasted_iota(jnp.int32, (8, W), 0)
    for n in range(nb):
        h_end, a_end = ends[n]
        h_in = h_ref[n]
        for r in range(1, 8):
            h_in = jnp.where(row == r, pltpu.roll(a_end * h_in + h_end, 1, axis=0), h_in)
        h_ref[n] = jnp.broadcast_to((a_end * h_in + h_end)[7:8, :], (8, W))
        for t in range(L):
            rows = slice(t * 8, (t + 1) * 8)
            hn_ref[pl.ds(t, 8, stride=L), :] = b_ref[n, rows, :] + a_ref[n, rows, :] * h_in
        sl = slice(n * W, (n + 1) * W)
        o_ref[:, sl] = (hn_ref[...] * gy_ref[:, sl].astype(F32)).astype(o_ref.dtype)


def _rg_lru(u_lx, gelu_y, conv_w, conv_b, w_a, b_a, w_x, b_x, lam, *, B, S, T=512, ct=512):
    N, C = u_lx.shape
    nb = ct // LRU_BLOCK_W
    ns = S // T
    row = lambda v: v.reshape(1, C).astype(F32)
    tile = pl.BlockSpec((T, ct), lambda b, c, s: (b * ns + s, c))
    vec = pl.BlockSpec((1, ct), lambda b, c, s: (0, c))
    wblk = pl.BlockSpec((nb, LRU_BLOCK_W, LRU_BLOCK_W), lambda b, c, s: (c, 0, 0))
    return pl.pallas_call(
        functools.partial(_lru_body, T=T, ct=ct),
        grid=(B, C // ct, ns),
        in_specs=[tile, tile, pl.BlockSpec((CONV_WIDTH, ct), lambda b, c, s: (0, c)), vec,
                  wblk, vec, wblk, vec, vec],
        out_specs=tile,
        out_shape=jax.ShapeDtypeStruct((N, C), BF16),
        scratch_shapes=[pltpu.VMEM((nb, T + 8, LRU_BLOCK_W), F32), pltpu.VMEM((T, LRU_BLOCK_W), F32),
                        pltpu.VMEM((nb, T, LRU_BLOCK_W), F32), pltpu.VMEM((nb, T, LRU_BLOCK_W), F32),
                        pltpu.VMEM((T, LRU_BLOCK_W), F32), pltpu.VMEM((nb, 8, LRU_BLOCK_W), F32)],
        compiler_params=_params(("parallel", "parallel", "arbitrary")),
        name="rg_lru",
    )(u_lx, gelu_y, conv_w.astype(F32), row(conv_b), w_a.astype(BF16), row(b_a),
      w_x.astype(BF16), row(b_x), row(lam))


def _cmp_body(x_ref, pos_ref, w1_ref, w2_ref, o_ref, *, nc, half):
    x = x_ref[...]
    first = jnp.dot(x, w1_ref[0:half, :], preferred_element_type=F32)
    second = jnp.dot(x, w1_ref[half:, :], preferred_element_type=F32)
    pos = jnp.broadcast_to(pos_ref[...], (8, 2 * half)).astype(BF16)
    cpos = jnp.dot(pos, w1_ref[...], preferred_element_type=F32)[0:1, :]
    hid = first + pltpu.roll(second, nc - 1, axis=0) + cpos
    o_ref[...] = jnp.dot(_gelu(hid).astype(BF16), w2_ref[...],
                         preferred_element_type=F32).astype(o_ref.dtype)


def _compress(x, t, pos, w1, w2, *, B, name):
    _, G, rows, half = x.shape
    nc = rows // B
    hidden = w1.shape[1]
    return pl.pallas_call(
        functools.partial(_cmp_body, nc=nc, half=half),
        grid=(B, G),
        in_specs=[pl.BlockSpec((None, None, nc, half), lambda b, g: (t, g, b, 0)),
                  pl.BlockSpec((1, 2 * half), lambda b, g: (0, 0)),
                  pl.BlockSpec((2 * half, hidden), lambda b, g: (0, 0)),
                  pl.BlockSpec((hidden, HEAD_DIM), lambda b, g: (0, 0))],
        out_specs=pl.BlockSpec((None, None, nc, HEAD_DIM), lambda b, g: (b, g, 0, 0)),
        out_shape=jax.ShapeDtypeStruct((B, G, nc, HEAD_DIM), BF16),
        compiler_params=_params(("parallel", "parallel")),
        name=name,
    )(x, pos.reshape(1, 2 * half).astype(F32), w1.astype(BF16), w2.astype(BF16))


def _qk(q, k):
    return lax.dot_general(q, k, (((1,), (1,)), ((), ())), preferred_element_type=F32)


def _nsa_body(q_ref, kc_ref, vc_ref, ks_ref, vs_ref, kw_ref, vw_ref, g_ref, o_ref,
              kaug_ref, vaug_ref, vwaug_ref, ovt_ref, qaug_ref, m_ref, acc_ref,
              sbuf_ref, part_ref, *, S, nc, n_sb, n_sel, tk):
    R = HPG * Q_TILE
    D2 = 2 * HEAD_DIM
    i = pl.program_id(2)
    s0 = i * Q_TILE

    @pl.when(i == 0)
    def _():
        key_blk = lax.broadcasted_iota(jnp.int32, (S, NB), 0) // SLC_BLOCK
        col = lax.broadcasted_iota(jnp.int32, (S, NB), 1)
        kaug_ref[:, 0:HEAD_DIM] = ks_ref[...]
        kaug_ref[:, HEAD_DIM:D2] = (key_blk == col).astype(BF16)
        vaug_ref[:, 0:HEAD_DIM] = vs_ref[...]
        vaug_ref[:, HEAD_DIM:D2] = jnp.ones((S, HEAD_DIM), BF16)
        vwaug_ref[:, 0:HEAD_DIM] = vw_ref[...]
        vwaug_ref[:, HEAD_DIM:D2] = jnp.ones((S, HEAD_DIM), BF16)
        ci = lax.broadcasted_iota(jnp.int32, (NB, nc), 1) * CMP_STRIDE
        si = lax.broadcasted_iota(jnp.int32, (NB, nc), 0) * SLC_BLOCK
        ovt_ref[...] = ((ci < si + SLC_BLOCK) & (si < ci + CMP_BLOCK)).astype(BF16)

    qb = q_ref[...]
    q4 = jnp.concatenate([qb[:, h * HEAD_DIM:(h + 1) * HEAD_DIM] for h in range(HPG)], axis=0)
    tq = s0 + lax.broadcasted_iota(jnp.int32, (Q_TILE, 1), 0)

    def softmax_pv(s, bias, v_aug):
        k = s.shape[-1]
        s3 = s.reshape(HPG, Q_TILE, k) + bias[None]
        e = jnp.exp2(s3 - jnp.max(s3, axis=-1, keepdims=True))
        return e, jnp.dot(e.reshape(R, k).astype(BF16), v_aug, preferred_element_type=F32)

    cend = lax.broadcasted_iota(jnp.int32, (1, nc), 1) * CMP_STRIDE + (CMP_BLOCK - 1)
    e_c, oc = softmax_pv(_qk(q4, kc_ref[...]), jnp.where(cend <= tq, 0.0, NEG), vc_ref[...])
    inv_c = 1.0 / jnp.maximum(jnp.sum(e_c, axis=-1, keepdims=True), 1e-30)
    any_c = (tq >= CMP_BLOCK - 1).astype(F32)
    o_cmp = oc.reshape(HPG, Q_TILE, HEAD_DIM) * (inv_c * any_c[None])

    span = WINDOW + Q_TILE
    w0 = pl.multiple_of(jnp.maximum(s0 - WINDOW, 0), Q_TILE)
    diff = tq - (w0 + lax.broadcasted_iota(jnp.int32, (1, span), 1))
    _, ow = softmax_pv(_qk(q4, kw_ref[pl.ds(w0, span), :]),
                       jnp.where((diff >= 0) & (diff < WINDOW), 0.0, NEG),
                       vwaug_ref[pl.ds(w0, span), :])
    o_win = (ow[:, :HEAD_DIM] / jnp.maximum(ow[:, HEAD_DIM:], 1e-30)
             ).reshape(HPG, Q_TILE, HEAD_DIM)
    gt = g_ref[...]
    for h in range(HPG):
        part_ref[:, h * HEAD_DIM:(h + 1) * HEAD_DIM] = (
            gt[:, h:h + 1] * o_cmp[h] + gt[:, 2 * HPG + h:2 * HPG + h + 1] * o_win[h])

    pc_sum = e_c[0] * inv_c[0]
    for h in range(1, HPG):
        pc_sum = pc_sum + e_c[h] * inv_c[h]
    pc_hi = pc_sum.astype(BF16)
    pc_lo = (pc_sum - pc_hi.astype(F32)).astype(BF16)
    ovt = ovt_ref[...]
    imp = _qk(ovt, pc_hi) + _qk(ovt, pc_lo)
    blk = lax.broadcasted_iota(jnp.int32, (NB, 1), 0)
    cur = (s0 + lax.broadcasted_iota(jnp.int32, (1, Q_TILE), 1)) // SLC_BLOCK
    forced = (blk == 0) | (blk == cur) | (blk == cur - 1)
    score = jnp.where(forced, KNOCKED_OUT, jnp.where(blk <= cur, imp, -FORCED_SCORE))
    if n_sb < NB:
        score = jnp.where(blk < n_sb, score, KNOCKED_OUT)
    blk_f = blk.astype(F32)
    sel = forced.astype(F32)
    for _ in range(n_sel - 3):
        mx = jnp.max(score, axis=0, keepdims=True)
        first = jnp.min(jnp.where(score == mx, blk_f, float(NB)), axis=0, keepdims=True)
        hit = blk_f == first
        sel = jnp.where(hit, 1.0, sel)
        score = jnp.where(hit, KNOCKED_OUT, score)
    def mask_columns(sel_nq):
        bias = ((sel_nq - 1.0) * (-NEG)).T.astype(BF16)
        return jnp.concatenate([bias] * HPG, axis=0)

    qaug_ref[:, 0:HEAD_DIM] = q4
    qaug_ref[:, HEAD_DIM:D2] = mask_columns(jnp.where(blk < s0 // SLC_BLOCK, sel, 0.0))

    own = pl.ds(pl.multiple_of(s0, Q_TILE), Q_TILE)
    lower = (lax.broadcasted_iota(jnp.int32, (Q_TILE, Q_TILE), 1)
             <= lax.broadcasted_iota(jnp.int32, (Q_TILE, Q_TILE), 0))
    q_own = jnp.concatenate([q4, mask_columns(sel)], axis=1)
    s_own = (_qk(q_own, kaug_ref[own, :]).reshape(HPG, Q_TILE, Q_TILE)
             + jnp.where(lower, 0.0, NEG)[None]).reshape(R, Q_TILE)
    m_own = jnp.broadcast_to(jnp.max(s_own, axis=-1, keepdims=True), (R, HEAD_DIM))
    m_ref[...] = m_own
    p_own = jnp.exp2(s_own - jnp.concatenate([m_own] * (Q_TILE // HEAD_DIM), axis=1))
    acc_ref[...] = jnp.dot(p_own.astype(BF16), vaug_ref[own, :],
                           preferred_element_type=F32)
    n_past = (s0 + tk - 1) // tk

    def scores(kt):
        return _qk(qaug_ref[...], kaug_ref[pl.ds(pl.multiple_of(kt * tk, tk), tk), :])

    def consume(kt):
        s = sbuf_ref[...]
        m_old = m_ref[...]
        m_new = jnp.maximum(m_old, jnp.max(s, axis=-1, keepdims=True))
        p = jnp.exp2(s - jnp.concatenate([m_new] * (tk // HEAD_DIM), axis=1)).astype(BF16)
        alpha = jnp.exp2(m_old - m_new)
        m_ref[...] = m_new
        acc_ref[...] = (jnp.concatenate([alpha, alpha], axis=1) * acc_ref[...]
                        + jnp.dot(p, vaug_ref[pl.ds(pl.multiple_of(kt * tk, tk), tk), :],
                                  preferred_element_type=F32))

    sbuf_ref[...] = scores(0)

    @pl.loop(0, jnp.maximum(n_past - 1, 0))
    def _(kt):
        s_next = scores(kt + 1)
        consume(kt)
        sbuf_ref[...] = s_next

    consume(jnp.maximum(n_past - 1, 0))
    acc_s = acc_ref[...]
    o_slc = (acc_s[:, :HEAD_DIM] / jnp.maximum(acc_s[:, HEAD_DIM:], 1e-30)
             ).reshape(HPG, Q_TILE, HEAD_DIM)

    gt = g_ref[...]
    for h in range(HPG):
        cols = slice(h * HEAD_DIM, (h + 1) * HEAD_DIM)
        o_ref[:, cols] = (part_ref[:, cols]
                          + gt[:, HPG + h:HPG + h + 1] * o_slc[h]).astype(o_ref.dtype)


def _nsa(q, kc, vc, kv4, gates, *, B, S, tk=1024):
    N = B * S
    nq = S // Q_TILE
    nc = kc.shape[2]
    n_sb = S // SLC_BLOCK
    assert n_sb <= NB and nc % HEAD_DIM == 0 and S % tk == 0
    G = N_KV_GROUPS
    gw = HPG * HEAD_DIM
    kv_spec = lambda t: pl.BlockSpec((None, None, S, HEAD_DIM), lambda b, g, i, t=t: (t, g, b, 0))
    cmp_spec = pl.BlockSpec((None, None, nc, HEAD_DIM), lambda b, g, i: (b, g, 0, 0))
    return pl.pallas_call(
        functools.partial(_nsa_body, S=S, nc=nc, n_sb=n_sb, n_sel=min(N_SELECT, n_sb), tk=tk),
        grid=(B, G, nq),
        in_specs=[pl.BlockSpec((Q_TILE, gw), lambda b, g, i: (b * nq + i, g)),
                  cmp_spec, cmp_spec, kv_spec(0), kv_spec(1), kv_spec(2), kv_spec(3),
                  pl.BlockSpec((Q_TILE, 128), lambda b, g, i: (b * nq + i, g))],
        out_specs=pl.BlockSpec((Q_TILE, gw), lambda b, g, i: (b * nq + i, g)),
        out_shape=jax.ShapeDtypeStruct((N, G * gw), BF16),
        scratch_shapes=[pltpu.VMEM((S, 2 * HEAD_DIM), BF16), pltpu.VMEM((S, 2 * HEAD_DIM), BF16),
                        pltpu.VMEM((S, 2 * HEAD_DIM), BF16),
                        pltpu.VMEM((NB, nc), BF16),
                        pltpu.VMEM((HPG * Q_TILE, 2 * HEAD_DIM), BF16),
                        pltpu.VMEM((HPG * Q_TILE, HEAD_DIM), F32),
                        pltpu.VMEM((HPG * Q_TILE, 2 * HEAD_DIM), F32),
                        pltpu.VMEM((HPG * Q_TILE, tk), F32),
                        pltpu.VMEM((Q_TILE, HPG * HEAD_DIM), F32)],
        compiler_params=_params(("parallel", "parallel", "arbitrary")),
        name="nsa_attention",
    )(q, kc, vc, kv4, kv4, kv4, kv4, gates)


def _mlp_body(x_ref, g_ref, wu_ref, wd_ref, o_ref, h_ref, acc_ref):
    f = pl.program_id(1)

    @pl.when(f == 0)
    def _():
        h_ref[...] = _rms(x_ref[...], g_ref[...]).astype(BF16)
        acc_ref[...] = jnp.zeros_like(acc_ref)

    up = jnp.dot(h_ref[...], wu_ref[...], preferred_element_type=F32)
    up = jnp.square(jnp.maximum(up, 0.0)).astype(BF16)
    acc_ref[...] += jnp.dot(up, wd_ref[...], preferred_element_type=F32)

    @pl.when(f == pl.num_programs(1) - 1)
    def _():
        o_ref[...] = x_ref[...] + acc_ref[...]


def _mlp(x, g, w_up, w_down, *, tm=512, tf=1024):
    M, D = x.shape
    FF = w_up.shape[1]
    return pl.pallas_call(
        _mlp_body,
        grid=(M // tm, FF // tf),
        in_specs=[pl.BlockSpec((tm, D), lambda i, f: (i, 0)),
                  pl.BlockSpec((1, D), lambda i, f: (0, 0)),
                  pl.BlockSpec((D, tf), lambda i, f: (0, f)),
                  pl.BlockSpec((tf, D), lambda i, f: (f, 0))],
        out_specs=pl.BlockSpec((tm, D), lambda i, f: (i, 0)),
        out_shape=jax.ShapeDtypeStruct((M, D), F32),
        scratch_shapes=[pltpu.VMEM((tm, D), BF16), pltpu.VMEM((tm, D), F32)],
        compiler_params=_params(("parallel", "arbitrary")),
        name="mlp",
    )(x, g.reshape(1, D).astype(F32), w_up, w_down)


def _ple_body(x_ref, p_ref, g_ref, wg_ref, wp_ref, gf_ref, o_ref, *, final_norm):
    x = x_ref[...]
    h = _rms(x, g_ref[...]).astype(BF16)
    gate = jax.nn.sigmoid(jnp.dot(h, wg_ref[...], preferred_element_type=F32))
    proj = jnp.dot(p_ref[...].astype(BF16), wp_ref[...], preferred_element_type=F32)
    y = x + gate * proj
    o_ref[...] = _rms(y, gf_ref[...]) if final_norm else y


def _ple(x, p, g, w_gate, w_proj, g_final, *, final_norm, tm=512):
    M, D = x.shape
    P = p.shape[1]
    row = lambda v: v.reshape(1, D).astype(F32)
    vec = pl.BlockSpec((1, D), lambda i: (0, 0))
    return pl.pallas_call(
        functools.partial(_ple_body, final_norm=final_norm),
        grid=(M // tm,),
        in_specs=[pl.BlockSpec((tm, D), lambda i: (i, 0)), pl.BlockSpec((tm, P), lambda i: (i, 0)),
                  vec, pl.BlockSpec((D, D), lambda i: (0, 0)), pl.BlockSpec((P, D), lambda i: (0, 0)),
                  vec],
        out_specs=pl.BlockSpec((tm, D), lambda i: (i, 0)),
        out_shape=jax.ShapeDtypeStruct((M, D), F32),
        compiler_params=_params(("parallel",)),
        name="ple_final",
    )(x, p, row(g), w_gate, w_proj, row(g_final))


def _layer(x, p, norm_mix_g, w_in, b_in, conv_w, conv_b, w_gate_a, b_gate_a, w_gate_x, b_gate_x,
           lru_lambda, w_lru_out, cmp_pos_k, cmp_w1_k, cmp_w2_k, cmp_pos_v, cmp_w1_v, cmp_w2_v,
           w_attn_out, w_o, norm_mlp_g, w_up, w_down, norm_ple_g, w_ple_gate, w_ple_proj,
           norm_final_g, *, B, S, final_norm):
    N, D = x.shape
    C = conv_w.shape[1]
    G = N_KV_GROUPS
    QW = N_HEADS * HEAD_DIM
    KVW = G * HEAD_DIM
    o_ly, o_q, o_kv, o_g, o_m = C, 2 * C, 2 * C + QW, 2 * C + QW + 6 * KVW, 2 * C + QW + 6 * KVW + 3 * N_HEADS
    wb = lambda lo, hi: (w_in[:, lo:hi].astype(BF16), b_in[lo:hi].reshape(1, hi - lo).astype(F32))
    proj = functools.partial(_mm, _norm_cast(x, norm_mix_g))

    w, b = wb(0, o_ly)
    u_lx = proj(w, lambda acc, b: acc + b, [(b, 'row', 0)], out_dtype=F32, name="in_lru_x")
    w, b = wb(o_ly, o_q)
    gelu_y = proj(w, lambda acc, b: _gelu(acc + b), [(b, 'row', 0)], out_dtype=BF16, name="in_lru_y")
    w, b = wb(o_q, o_kv)
    q = proj(w, lambda acc, b: (acc + b) * (HEAD_DIM ** -0.5 * LOG2E), [(b, 'row', 0)],
             out_dtype=BF16, name="in_q")
    w, b = wb(o_kv, o_kv + 2 * KVW)
    kv_cmp = proj(w, lambda acc, b: acc + b, [(b, 'row', 0)], out_dtype=BF16, tn=KVW, split=G,
                  pack=CMP_STRIDE, name="in_kv_cmp")
    w, b = wb(o_kv + 2 * KVW, o_g)
    kv4 = proj(w, lambda acc, b: acc + b, [(b, 'row', 0)], out_dtype=BF16, tn=KVW, split=G,
               name="in_kv")
    wg = w_in[:, o_g:o_m].reshape(D, 3, G, HPG).transpose(0, 2, 1, 3).reshape(D, G, 3 * HPG)
    wg = jnp.pad(wg, ((0, 0), (0, 0), (0, 128 - 3 * HPG))).reshape(D, G * 128).astype(BF16)
    bg = b_in[o_g:o_m].reshape(3, G, HPG).transpose(1, 0, 2).reshape(G, 3 * HPG)
    bg = jnp.pad(bg, ((0, 0), (0, 128 - 3 * HPG))).reshape(1, G * 128).astype(F32)
    gates = proj(wg, lambda acc, b: jax.nn.sigmoid(acc + b), [(bg, 'row', 0)], out_dtype=F32,
                 name="in_gates")
    w, b = wb(o_m, o_m + 2 * D)
    gm = proj(w, lambda acc, b: jax.nn.sigmoid(acc + b), [(b, 'row', 0)], out_dtype=BF16,
              name="in_merge_gates")

    a_in = _rg_lru(u_lx, gelu_y, conv_w, conv_b, w_gate_a, b_gate_a, w_gate_x, b_gate_x,
                   lru_lambda, B=B, S=S)
    nc = S // CMP_STRIDE
    kc = _compress(kv_cmp, 0, cmp_pos_k, cmp_w1_k, cmp_w2_k, B=B, name="compress_k")
    vc = _compress(kv_cmp, 1, cmp_pos_v, cmp_w1_v, cmp_w2_v, B=B, name="compress_v")
    o_nsa = _nsa(q, kc, vc, kv4, gates, B=B, S=S)

    mixed = _merge(a_in, o_nsa, w_lru_out.astype(BF16), w_attn_out.astype(BF16), gm)
    x1 = _mm(mixed, w_o.astype(BF16), lambda acc, xr: xr + acc, [(x, 'tile', 0)],
             out_dtype=F32, name="out_proj")
    x2 = _mlp(x1, norm_mlp_g, w_up.astype(BF16), w_down.astype(BF16))
    return _ple(x2, p, norm_ple_g, w_ple_gate.astype(BF16), w_ple_proj.astype(BF16),
                norm_final_g, final_norm=final_norm)


def kernel(x, p, norm_mix_g, w_in, b_in, conv_w, conv_b, w_gate_a, b_gate_a, w_gate_x, b_gate_x, lru_lambda, w_lru_out, cmp_pos_k, cmp_w1_k, cmp_w2_k, cmp_pos_v, cmp_w1_v, cmp_w2_v, w_attn_out, w_o, norm_mlp_g, w_up, w_down, norm_ple_g, w_ple_gate, w_ple_proj, norm_final_g):
    B, S, D = x.shape
    depth = w_in.shape[0]
    assert S % 512 == 0 and S >= WINDOW + Q_TILE and D % 512 == 0
    xs = x.reshape(B * S, D)
    for i in range(depth):
        per_layer = [t[i] for t in (
            norm_mix_g, w_in, b_in, conv_w, conv_b, w_gate_a, b_gate_a, w_gate_x, b_gate_x,
            lru_lambda, w_lru_out, cmp_pos_k, cmp_w1_k, cmp_w2_k, cmp_pos_v, cmp_w1_v, cmp_w2_v,
            w_attn_out, w_o, norm_mlp_g, w_up, w_down, norm_ple_g, w_ple_gate, w_ple_proj)]
        xs = _layer(xs, p[i].reshape(B * S, -1), *per_layer, norm_final_g, B=B, S=S,
                    final_norm=i == depth - 1)
    return xs.reshape(B, S, D)
```

```python
import functools

import jax
import jax.numpy as jnp
from jax import lax
from jax.experimental import pallas as pl
from jax.experimental.pallas import tpu as pltpu

F32 = jnp.float32
BF16 = jnp.bfloat16

LRU_BLOCK_W = 128
CONV_WIDTH = 4
LRU_C = 8.0
N_HEADS = 16
N_KV_GROUPS = 4
HPG = N_HEADS // N_KV_GROUPS
HEAD_DIM = 128
CMP_STRIDE = 16
CMP_BLOCK = 32
SLC_BLOCK = 64
N_SELECT = 16
WINDOW = 512
Q_TILE = 256
EPS = 1e-6
NEG = -1e30
FORCED_SCORE = 1e4
KNOCKED_OUT = -3e38
NB = 128
LOG2E = 1.4426950408889634

VMEM_LIMIT = 56 * 1024 * 1024


def _params(sem):
    return pltpu.CompilerParams(dimension_semantics=sem, vmem_limit_bytes=VMEM_LIMIT)


def _rms(xf, g):
    return xf * lax.rsqrt(jnp.mean(xf * xf, axis=-1, keepdims=True) + EPS) * g


def _gelu(x):
    return jax.nn.gelu(x)


def _norm_body(x_ref, g_ref, o_ref):
    o_ref[...] = _rms(x_ref[...], g_ref[...]).astype(o_ref.dtype)


def _norm_cast(x, g, *, tm=1024):
    M, D = x.shape
    return pl.pallas_call(
        _norm_body,
        grid=(M // tm,),
        in_specs=[pl.BlockSpec((tm, D), lambda i: (i, 0)), pl.BlockSpec((1, D), lambda i: (0, 0))],
        out_specs=pl.BlockSpec((tm, D), lambda i: (i, 0)),
        out_shape=jax.ShapeDtypeStruct((M, D), BF16),
        compiler_params=_params(("parallel",)),
        name="norm_mix",
    )(x, g.reshape(1, D).astype(F32))


def _mm_body(*refs, n_extra, epi, split, pack):
    a_ref, w_ref = refs[0], refs[1]
    extras = refs[2:2 + n_extra]
    o_ref = refs[2 + n_extra]
    acc = jnp.dot(a_ref[...], w_ref[...], preferred_element_type=F32)
    res = epi(acc, *[e[...] for e in extras])
    if split and pack:
        stage_ref = refs[3 + n_extra]
        rows = res.shape[0] // pack
        for g in range(split):
            stage_ref[...] = res[:, g * HEAD_DIM:(g + 1) * HEAD_DIM]
            for t in range(pack):
                o_ref[g, :, t * HEAD_DIM:(t + 1) * HEAD_DIM] = (
                    stage_ref[pl.ds(t, rows, stride=pack), :].astype(o_ref.dtype))
    elif split:
        for g in range(split):
            o_ref[g] = res[:, g * HEAD_DIM:(g + 1) * HEAD_DIM].astype(o_ref.dtype)
    else:
        o_ref[...] = res.astype(o_ref.dtype)


def _mm(a, w, epi, extras=(), *, out_dtype, tm=1024, tn=1024, split=0, pack=0, name):
    M, K = a.shape
    N = w.shape[1]
    tn = min(tn, N)
    grid = (M // tm, N // tn)
    in_specs = [pl.BlockSpec((tm, K), lambda i, j: (i, 0)),
                pl.BlockSpec((K, tn), lambda i, j: (0, j))]
    args = [a, w]
    for arr, kind, off in extras:
        if kind == 'row':
            in_specs.append(pl.BlockSpec((1, tn), lambda i, j, off=off: (0, j + off)))
        else:
            in_specs.append(pl.BlockSpec((tm, tn), lambda i, j, off=off: (i, j + off)))
        args.append(arr)
    scratch = []
    if split and pack:
        out_shape = jax.ShapeDtypeStruct((N // tn, split, M // pack, pack * HEAD_DIM), out_dtype)
        out_spec = pl.BlockSpec((None, split, tm // pack, pack * HEAD_DIM),
                                lambda i, j: (j, 0, i, 0))
        scratch = [pltpu.VMEM((tm, HEAD_DIM), F32)]
    elif split:
        out_shape = jax.ShapeDtypeStruct((N // tn, split, M, HEAD_DIM), out_dtype)
        out_spec = pl.BlockSpec((None, split, tm, HEAD_DIM), lambda i, j: (j, 0, i, 0))
    else:
        out_shape = jax.ShapeDtypeStruct((M, N), out_dtype)
        out_spec = pl.BlockSpec((tm, tn), lambda i, j: (i, j))
    return pl.pallas_call(
        functools.partial(_mm_body, n_extra=len(extras), epi=epi, split=split, pack=pack),
        grid=grid, in_specs=in_specs, out_specs=out_spec, out_shape=out_shape,
        scratch_shapes=scratch,
        compiler_params=_params(("parallel", "arbitrary")),
        name=name,
    )(*args)


def _merge_body(a_ref, b_ref, wa_ref, wb_ref, ga_ref, gb_ref, o_ref):
    ya = jnp.dot(a_ref[...], wa_ref[...], preferred_element_type=F32)
    yb = jnp.dot(b_ref[...], wb_ref[...], preferred_element_type=F32)
    o_ref[...] = (ga_ref[...].astype(F32) * ya + gb_ref[...].astype(F32) * yb).astype(o_ref.dtype)


def _merge(a, b, wa, wb, gates, *, tm=1024, tn=512):
    M, K = a.shape
    N = wa.shape[1]
    nt = N // tn
    lhs = pl.BlockSpec((tm, K), lambda i, j: (i, 0))
    wsp = pl.BlockSpec((K, tn), lambda i, j: (0, j))
    return pl.pallas_call(
        _merge_body,
        grid=(M // tm, nt),
        in_specs=[lhs, lhs, wsp, wsp, pl.BlockSpec((tm, tn), lambda i, j: (i, j)),
                  pl.BlockSpec((tm, tn), lambda i, j: (i, j + nt))],
        out_specs=pl.BlockSpec((tm, tn), lambda i, j: (i, j)),
        out_shape=jax.ShapeDtypeStruct((M, N), BF16),
        compiler_params=_params(("parallel", "arbitrary")),
        name="mixer_merge",
    )(a, b, wa, wb, gates, gates)


def _lru_body(u_ref, gy_ref, cw_ref, cb_ref, wa_ref, ba_ref, wx_ref, bx_ref, lam_ref,
              o_ref, xe_ref, xc_ref, a_ref, b_ref, hn_ref, h_ref, *, T, ct):
    L = T // 8
    W = LRU_BLOCK_W
    nb = ct // W
    s = pl.program_id(2)

    @pl.when(s == 0)
    def _():
        xe_ref[:, 0:8, :] = jnp.zeros((nb, 8, W), F32)
        h_ref[...] = jnp.zeros((nb, 8, W), F32)

    lam = lam_ref[...]
    neg_softplus = -(jnp.maximum(-lam, 0.0) + jnp.log1p(jnp.exp(-jnp.abs(lam))))
    log_a_scale = LRU_C * neg_softplus
    for n in range(nb):
        sl = slice(n * W, (n + 1) * W)
        xe_ref[n, 8:, :] = u_ref[:, sl]

        def seg_rows(off):
            return xe_ref[n, pl.ds(off, 8, stride=L), :]

        taps = [jnp.broadcast_to(cw_ref[k:k + 1, sl], (8, W)) for k in range(CONV_WIDTH)]
        bias = jnp.broadcast_to(cb_ref[:, sl], (8, W))
        prev = [seg_rows(8 - CONV_WIDTH + 1 + k) for k in range(CONV_WIDTH - 1)]
        for t in range(L):
            win = prev + [seg_rows(8 + t)]
            y = bias
            for k in range(CONV_WIDTH):
                y = y + taps[k] * win[k]
            xc_ref[t * 8:(t + 1) * 8, :] = y
            prev = win[1:]
        xe_ref[n, 0:8, :] = xe_ref[n, T:T + 8, :]

        xc = xc_ref[...]
        xb = xc.astype(BF16)
        r = jax.nn.sigmoid(jnp.dot(xb, wa_ref[n], preferred_element_type=F32) + ba_ref[:, sl])
        i = jax.nn.sigmoid(jnp.dot(xb, wx_ref[n], preferred_element_type=F32) + bx_ref[:, sl])
        a = jnp.exp(r * log_a_scale[:, sl])
        a_ref[n] = a
        b_ref[n] = jnp.sqrt(jnp.maximum(1.0 - a * a, 0.0)) * (i * xc)

    def step(t, carry):
        rows = pl.ds(pl.multiple_of(t * 8, 8), 8)
        out = []
        for n in range(nb):
            h, decay = carry[n]
            a_t = a_ref[n, rows, :]
            h = a_t * h + b_ref[n, rows, :]
            decay = a_t * decay
            b_ref[n, rows, :] = h
            a_ref[n, rows, :] = decay
            out.append((h, decay))
        return tuple(out)

    init = tuple((jnp.zeros((8, W), F32), jnp.ones((8, W), F32)) for _ in range(nb))
    ends = lax.fori_loop(0, L, step, init, unroll=8)
    row = lax.broadcasted_iota(jnp.int32, (8, W), 0)
    for n in range(nb):
        h_end, a_end = ends[n]
        h_in = h_ref[n]
        for r in range(1, 8):
            h_in = jnp.where(row == r, pltpu.roll(a_end * h_in + h_end, 1, axis=0), h_in)
        h_ref[n] = jnp.broadcast_to((a_end * h_in + h_end)[7:8, :], (8, W))
        for t in range(L):
            rows = slice(t * 8, (t + 1) * 8)
            hn_ref[pl.ds(t, 8, stride=L), :] = b_ref[n, rows, :] + a_ref[n, rows, :] * h_in
        sl = slice(n * W, (n + 1) * W)
        o_ref[:, sl] = (hn_ref[...] * gy_ref[:, sl].astype(F32)).astype(o_ref.dtype)


def _rg_lru(u_lx, gelu_y, conv_w, conv_b, w_a, b_a, w_x, b_x, lam, *, B, S, T=512, ct=512):
    N, C = u_lx.shape
    nb = ct // LRU_BLOCK_W
    ns = S // T
    row = lambda v: v.reshape(1, C).astype(F32)
    tile = pl.BlockSpec((T, ct), lambda b, c, s: (b * ns + s, c))
    vec = pl.BlockSpec((1, ct), lambda b, c, s: (0, c))
    wblk = pl.BlockSpec((nb, LRU_BLOCK_W, LRU_BLOCK_W), lambda b, c, s: (c, 0, 0))
    return pl.pallas_call(
        functools.partial(_lru_body, T=T, ct=ct),
        grid=(B, C // ct, ns),
        in_specs=[tile, tile, pl.BlockSpec((CONV_WIDTH, ct), lambda b, c, s: (0, c)), vec,
                  wblk, vec, wblk, vec, vec],
        out_specs=tile,
        out_shape=jax.ShapeDtypeStruct((N, C), BF16),
        scratch_shapes=[pltpu.VMEM((nb, T + 8, LRU_BLOCK_W), F32), pltpu.VMEM((T, LRU_BLOCK_W), F32),
                        pltpu.VMEM((nb, T, LRU_BLOCK_W), F32), pltpu.VMEM((nb, T, LRU_BLOCK_W), F32),
                        pltpu.VMEM((T, LRU_BLOCK_W), F32), pltpu.VMEM((nb, 8, LRU_BLOCK_W), F32)],
        compiler_params=_params(("parallel", "parallel", "arbitrary")),
        name="rg_lru",
    )(u_lx, gelu_y, conv_w.astype(F32), row(conv_b), w_a.astype(BF16), row(b_a),
      w_x.astype(BF16), row(b_x), row(lam))


def _cmp_body(x_ref, pos_ref, w1_ref, w2_ref, o_ref, *, nc, half):
    x = x_ref[...]
    first = jnp.dot(x, w1_ref[0:half, :], preferred_element_type=F32)
    second = jnp.dot(x, w1_ref[half:, :], preferred_element_type=F32)
    pos = jnp.broadcast_to(pos_ref[...], (8, 2 * half)).astype(BF16)
    cpos = jnp.dot(pos, w1_ref[...], preferred_element_type=F32)[0:1, :]
    hid = first + pltpu.roll(second, nc - 1, axis=0) + cpos
    o_ref[...] = jnp.dot(_gelu(hid).astype(BF16), w2_ref[...],
                         preferred_element_type=F32).astype(o_ref.dtype)


def _compress(x, t, pos, w1, w2, *, B, name):
    _, G, rows, half = x.shape
    nc = rows // B
    hidden = w1.shape[1]
    return pl.pallas_call(
        functools.partial(_cmp_body, nc=nc, half=half),
        grid=(B, G),
        in_specs=[pl.BlockSpec((None, None, nc, half), lambda b, g: (t, g, b, 0)),
                  pl.BlockSpec((1, 2 * half), lambda b, g: (0, 0)),
                  pl.BlockSpec((2 * half, hidden), lambda b, g: (0, 0)),
                  pl.BlockSpec((hidden, HEAD_DIM), lambda b, g: (0, 0))],
        out_specs=pl.BlockSpec((None, None, nc, HEAD_DIM), lambda b, g: (b, g, 0, 0)),
        out_shape=jax.ShapeDtypeStruct((B, G, nc, HEAD_DIM), BF16),
        compiler_params=_params(("parallel", "parallel")),
        name=name,
    )(x, pos.reshape(1, 2 * half).astype(F32), w1.astype(BF16), w2.astype(BF16))


def _qk(q, k):
    return lax.dot_general(q, k, (((1,), (1,)), ((), ())), preferred_element_type=F32)


def _nsa_body(q_ref, kc_ref, vc_ref, ks_ref, vs_ref, kw_ref, vw_ref, g_ref, o_ref,
              kaug_ref, vaug_ref, vwaug_ref, ovt_ref, qaug_ref, m_ref, acc_ref,
              sbuf_ref, part_ref, *, S, nc, n_sb, n_sel, tk):
    R = HPG * Q_TILE
    D2 = 2 * HEAD_DIM
    i = pl.program_id(2)
    s0 = i * Q_TILE

    @pl.when(i == 0)
    def _():
        key_blk = lax.broadcasted_iota(jnp.int32, (S, NB), 0) // SLC_BLOCK
        col = lax.broadcasted_iota(jnp.int32, (S, NB), 1)
        kaug_ref[:, 0:HEAD_DIM] = ks_ref[...]
        kaug_ref[:, HEAD_DIM:D2] = (key_blk == col).astype(BF16)
        vaug_ref[:, 0:HEAD_DIM] = vs_ref[...]
        vaug_ref[:, HEAD_DIM:D2] = jnp.ones((S, HEAD_DIM), BF16)
        vwaug_ref[:, 0:HEAD_DIM] = vw_ref[...]
        vwaug_ref[:, HEAD_DIM:D2] = jnp.ones((S, HEAD_DIM), BF16)
        ci = lax.broadcasted_iota(jnp.int32, (NB, nc), 1) * CMP_STRIDE
        si = lax.broadcasted_iota(jnp.int32, (NB, nc), 0) * SLC_BLOCK
        ovt_ref[...] = ((ci < si + SLC_BLOCK) & (si < ci + CMP_BLOCK)).astype(BF16)

    qb = q_ref[...]
    q4 = jnp.concatenate([qb[:, h * HEAD_DIM:(h + 1) * HEAD_DIM] for h in range(HPG)], axis=0)
    tq = s0 + lax.broadcasted_iota(jnp.int32, (Q_TILE, 1), 0)

    def softmax_pv(s, bias, v_aug):
        k = s.shape[-1]
        s3 = s.reshape(HPG, Q_TILE, k) + bias[None]
        e = jnp.exp2(s3 - jnp.max(s3, axis=-1, keepdims=True))
        return e, jnp.dot(e.reshape(R, k).astype(BF16), v_aug, preferred_element_type=F32)

    cend = lax.broadcasted_iota(jnp.int32, (1, nc), 1) * CMP_STRIDE + (CMP_BLOCK - 1)
    s_c = _qk(q4, kc_ref[...]).reshape(HPG, Q_TILE, nc) + jnp.where(cend <= tq, 0.0, NEG)[None]
    e_c = jnp.exp2(s_c - jnp.max(s_c, axis=-1, keepdims=True))
    inv_c = 1.0 / jnp.maximum(jnp.sum(e_c, axis=-1, keepdims=True), 1e-30)

    pc_sum = e_c[0] * inv_c[0]
    for h in range(1, HPG):
        pc_sum = pc_sum + e_c[h] * inv_c[h]
    pc_hi = pc_sum.astype(BF16)
    pc_lo = (pc_sum - pc_hi.astype(F32)).astype(BF16)
    ovt = ovt_ref[...]
    imp = _qk(ovt, pc_hi) + _qk(ovt, pc_lo)
    blk = lax.broadcasted_iota(jnp.int32, (NB, 1), 0)
    cur = (s0 + lax.broadcasted_iota(jnp.int32, (1, Q_TILE), 1)) // SLC_BLOCK
    forced = (blk == 0) | (blk == cur) | (blk == cur - 1)
    score = jnp.where(forced, KNOCKED_OUT, jnp.where(blk <= cur, imp, -FORCED_SCORE))
    if n_sb < NB:
        score = jnp.where(blk < n_sb, score, KNOCKED_OUT)
    blk_f = blk.astype(F32)
    sel = forced.astype(F32)
    def pick(score, sel):
        mx = jnp.max(score, axis=0, keepdims=True)
        first = jnp.min(jnp.where(score == mx, blk_f, float(NB)), axis=0, keepdims=True)
        hit = blk_f == first
        return jnp.where(hit, KNOCKED_OUT, score), jnp.where(hit, 1.0, sel)

    for _ in range((n_sel - 3) // 2):
        score, sel = pick(score, sel)

    oc = jnp.dot(e_c.reshape(R, nc).astype(BF16), vc_ref[...], preferred_element_type=F32)
    any_c = (tq >= CMP_BLOCK - 1).astype(F32)
    o_cmp = oc.reshape(HPG, Q_TILE, HEAD_DIM) * (inv_c * any_c[None])
    span = WINDOW + Q_TILE
    w0 = pl.multiple_of(jnp.maximum(s0 - WINDOW, 0), Q_TILE)
    diff = tq - (w0 + lax.broadcasted_iota(jnp.int32, (1, span), 1))
    _, ow = softmax_pv(_qk(q4, kw_ref[pl.ds(w0, span), :]),
                       jnp.where((diff >= 0) & (diff < WINDOW), 0.0, NEG),
                       vwaug_ref[pl.ds(w0, span), :])
    o_win = (ow[:, :HEAD_DIM] / jnp.maximum(ow[:, HEAD_DIM:], 1e-30)
             ).reshape(HPG, Q_TILE, HEAD_DIM)
    gt = g_ref[...]
    for h in range(HPG):
        part_ref[:, h * HEAD_DIM:(h + 1) * HEAD_DIM] = (
            gt[:, h:h + 1] * o_cmp[h] + gt[:, 2 * HPG + h:2 * HPG + h + 1] * o_win[h])

    for _ in range((n_sel - 3) - (n_sel - 3) // 2):
        score, sel = pick(score, sel)

    def mask_columns(sel_nq):
        bias = ((sel_nq - 1.0) * (-NEG)).T.astype(BF16)
        return jnp.concatenate([bias] * HPG, axis=0)

    qaug_ref[:, 0:HEAD_DIM] = q4
    qaug_ref[:, HEAD_DIM:D2] = mask_columns(jnp.where(blk < s0 // SLC_BLOCK, sel, 0.0))

    own = pl.ds(pl.multiple_of(s0, Q_TILE), Q_TILE)
    lower = (lax.broadcasted_iota(jnp.int32, (Q_TILE, Q_TILE), 1)
             <= lax.broadcasted_iota(jnp.int32, (Q_TILE, Q_TILE), 0))
    q_own = jnp.concatenate([q4, mask_columns(sel)], axis=1)
    s_own = (_qk(q_own, kaug_ref[own, :]).reshape(HPG, Q_TILE, Q_TILE)
             + jnp.where(lower, 0.0, NEG)[None]).reshape(R, Q_TILE)
    m_own = jnp.broadcast_to(jnp.max(s_own, axis=-1, keepdims=True), (R, HEAD_DIM))
    m_ref[...] = m_own
    p_own = jnp.exp2(s_own - jnp.concatenate([m_own] * (Q_TILE // HEAD_DIM), axis=1))
    acc_ref[...] = jnp.dot(p_own.astype(BF16), vaug_ref[own, :],
                           preferred_element_type=F32)
    n_past = (s0 + tk - 1) // tk

    def scores(kt):
        return _qk(qaug_ref[...], kaug_ref[pl.ds(pl.multiple_of(kt * tk, tk), tk), :])

    def consume(kt):
        s = sbuf_ref[...]
        m_old = m_ref[...]
        m_new = jnp.maximum(m_old, jnp.max(s, axis=-1, keepdims=True))
        p = jnp.exp2(s - jnp.concatenate([m_new] * (tk // HEAD_DIM), axis=1)).astype(BF16)
        alpha = jnp.exp2(m_old - m_new)
        m_ref[...] = m_new
        acc_ref[...] = (jnp.concatenate([alpha, alpha], axis=1) * acc_ref[...]
                        + jnp.dot(p, vaug_ref[pl.ds(pl.multiple_of(kt * tk, tk), tk), :],
                                  preferred_element_type=F32))

    sbuf_ref[...] = scores(0)

    @pl.loop(0, jnp.maximum(n_past - 1, 0))
    def _(kt):
        s_next = scores(kt + 1)
        consume(kt)
        sbuf_ref[...] = s_next

    consume(jnp.maximum(n_past - 1, 0))
    acc_s = acc_ref[...]
    o_slc = (acc_s[:, :HEAD_DIM] / jnp.maximum(acc_s[:, HEAD_DIM:], 1e-30)
             ).reshape(HPG, Q_TILE, HEAD_DIM)

    gt = g_ref[...]
    for h in range(HPG):
        cols = slice(h * HEAD_DIM, (h + 1) * HEAD_DIM)
        o_ref[:, cols] = (part_ref[:, cols]
                          + gt[:, HPG + h:HPG + h + 1] * o_slc[h]).astype(o_ref.dtype)


def _nsa(q, kc, vc, kv4, gates, *, B, S, tk=1024):
    N = B * S
    nq = S // Q_TILE
    nc = kc.shape[2]
    n_sb = S // SLC_BLOCK
    assert n_sb <= NB and nc % HEAD_DIM == 0 and S % tk == 0
    G = N_KV_GROUPS
    gw = HPG * HEAD_DIM
    kv_spec = lambda t: pl.BlockSpec((None, None, S, HEAD_DIM), lambda b, g, i, t=t: (t, g, b, 0))
    cmp_spec = pl.BlockSpec((None, None, nc, HEAD_DIM), lambda b, g, i: (b, g, 0, 0))
    return pl.pallas_call(
        functools.partial(_nsa_body, S=S, nc=nc, n_sb=n_sb, n_sel=min(N_SELECT, n_sb), tk=tk),
        grid=(B, G, nq),
        in_specs=[pl.BlockSpec((Q_TILE, gw), lambda b, g, i: (b * nq + i, g)),
                  cmp_spec, cmp_spec, kv_spec(0), kv_spec(1), kv_spec(2), kv_spec(3),
                  pl.BlockSpec((Q_TILE, 128), lambda b, g, i: (b * nq + i, g))],
        out_specs=pl.BlockSpec((Q_TILE, gw), lambda b, g, i: (b * nq + i, g)),
        out_shape=jax.ShapeDtypeStruct((N, G * gw), BF16),
        scratch_shapes=[pltpu.VMEM((S, 2 * HEAD_DIM), BF16), pltpu.VMEM((S, 2 * HEAD_DIM), BF16),
                        pltpu.VMEM((S, 2 * HEAD_DIM), BF16),
                        pltpu.VMEM((NB, nc), BF16),
                        pltpu.VMEM((HPG * Q_TILE, 2 * HEAD_DIM), BF16),
                        pltpu.VMEM((HPG * Q_TILE, HEAD_DIM), F32),
                        pltpu.VMEM((HPG * Q_TILE, 2 * HEAD_DIM), F32),
                        pltpu.VMEM((HPG * Q_TILE, tk), F32),
                        pltpu.VMEM((Q_TILE, HPG * HEAD_DIM), F32)],
        compiler_params=_params(("parallel", "parallel", "arbitrary")),
        name="nsa_attention",
    )(q, kc, vc, kv4, kv4, kv4, kv4, gates)


def _mlp_body(x_ref, g_ref, wu_ref, wd_ref, o_ref, h_ref, acc_ref):
    f = pl.program_id(1)

    @pl.when(f == 0)
    def _():
        h_ref[...] = _rms(x_ref[...], g_ref[...]).astype(BF16)
        acc_ref[...] = jnp.zeros_like(acc_ref)

    up = jnp.dot(h_ref[...], wu_ref[...], preferred_element_type=F32)
    up = jnp.square(jnp.maximum(up, 0.0)).astype(BF16)
    acc_ref[...] += jnp.dot(up, wd_ref[...], preferred_element_type=F32)

    @pl.when(f == pl.num_programs(1) - 1)
    def _():
        o_ref[...] = x_ref[...] + acc_ref[...]


def _mlp(x, g, w_up, w_down, *, tm=512, tf=1024):
    M, D = x.shape
    FF = w_up.shape[1]
    return pl.pallas_call(
        _mlp_body,
        grid=(M // tm, FF // tf),
        in_specs=[pl.BlockSpec((tm, D), lambda i, f: (i, 0)),
                  pl.BlockSpec((1, D), lambda i, f: (0, 0)),
                  pl.BlockSpec((D, tf), lambda i, f: (0, f)),
                  pl.BlockSpec((tf, D), lambda i, f: (f, 0))],
        out_specs=pl.BlockSpec((tm, D), lambda i, f: (i, 0)),
        out_shape=jax.ShapeDtypeStruct((M, D), F32),
        scratch_shapes=[pltpu.VMEM((tm, D), BF16), pltpu.VMEM((tm, D), F32)],
        compiler_params=_params(("parallel", "arbitrary")),
        name="mlp",
    )(x, g.reshape(1, D).astype(F32), w_up, w_down)


def _ple_body(x_ref, p_ref, g_ref, wg_ref, wp_ref, gf_ref, o_ref, *, final_norm):
    x = x_ref[...]
    h = _rms(x, g_ref[...]).astype(BF16)
    gate = jax.nn.sigmoid(jnp.dot(h, wg_ref[...], preferred_element_type=F32))
    proj = jnp.dot(p_ref[...].astype(BF16), wp_ref[...], preferred_element_type=F32)
    y = x + gate * proj
    o_ref[...] = _rms(y, gf_ref[...]) if final_norm else y


def _ple(x, p, g, w_gate, w_proj, g_final, *, final_norm, tm=512):
    M, D = x.shape
    P = p.shape[1]
    row = lambda v: v.reshape(1, D).astype(F32)
    vec = pl.BlockSpec((1, D), lambda i: (0, 0))
    return pl.pallas_call(
        functools.partial(_ple_body, final_norm=final_norm),
        grid=(M // tm,),
        in_specs=[pl.BlockSpec((tm, D), lambda i: (i, 0)), pl.BlockSpec((tm, P), lambda i: (i, 0)),
                  vec, pl.BlockSpec((D, D), lambda i: (0, 0)), pl.BlockSpec((P, D), lambda i: (0, 0)),
                  vec],
        out_specs=pl.BlockSpec((tm, D), lambda i: (i, 0)),
        out_shape=jax.ShapeDtypeStruct((M, D), F32),
        compiler_params=_params(("parallel",)),
        name="ple_final",
    )(x, p, row(g), w_gate, w_proj, row(g_final))


def _layer(x, p, norm_mix_g, w_in, b_in, conv_w, conv_b, w_gate_a, b_gate_a, w_gate_x, b_gate_x,
           lru_lambda, w_lru_out, cmp_pos_k, cmp_w1_k, cmp_w2_k, cmp_pos_v, cmp_w1_v, cmp_w2_v,
           w_attn_out, w_o, norm_mlp_g, w_up, w_down, norm_ple_g, w_ple_gate, w_ple_proj,
           norm_final_g, *, B, S, final_norm):
    N, D = x.shape
    C = conv_w.shape[1]
    G = N_KV_GROUPS
    QW = N_HEADS * HEAD_DIM
    KVW = G * HEAD_DIM
    o_ly, o_q, o_kv, o_g, o_m = C, 2 * C, 2 * C + QW, 2 * C + QW + 6 * KVW, 2 * C + QW + 6 * KVW + 3 * N_HEADS
    wb = lambda lo, hi: (w_in[:, lo:hi].astype(BF16), b_in[lo:hi].reshape(1, hi - lo).astype(F32))
    proj = functools.partial(_mm, _norm_cast(x, norm_mix_g))

    w, b = wb(0, o_ly)
    u_lx = proj(w, lambda acc, b: acc + b, [(b, 'row', 0)], out_dtype=F32, name="in_lru_x")
    w, b = wb(o_ly, o_q)
    gelu_y = proj(w, lambda acc, b: _gelu(acc + b), [(b, 'row', 0)], out_dtype=BF16, name="in_lru_y")
    w, b = wb(o_q, o_kv)
    q = proj(w, lambda acc, b: (acc + b) * (HEAD_DIM ** -0.5 * LOG2E), [(b, 'row', 0)],
             out_dtype=BF16, name="in_q")
    w, b = wb(o_kv, o_kv + 2 * KVW)
    kv_cmp = proj(w, lambda acc, b: acc + b, [(b, 'row', 0)], out_dtype=BF16, tn=KVW, split=G,
                  pack=CMP_STRIDE, name="in_kv_cmp")
    w, b = wb(o_kv + 2 * KVW, o_g)
    kv4 = proj(w, lambda acc, b: acc + b, [(b, 'row', 0)], out_dtype=BF16, tn=KVW, split=G,
               name="in_kv")
    wg = w_in[:, o_g:o_m].reshape(D, 3, G, HPG).transpose(0, 2, 1, 3).reshape(D, G, 3 * HPG)
    wg = jnp.pad(wg, ((0, 0), (0, 0), (0, 128 - 3 * HPG))).reshape(D, G * 128).astype(BF16)
    bg = b_in[o_g:o_m].reshape(3, G, HPG).transpose(1, 0, 2).reshape(G, 3 * HPG)
    bg = jnp.pad(bg, ((0, 0), (0, 128 - 3 * HPG))).reshape(1, G * 128).astype(F32)
    gates = proj(wg, lambda acc, b: jax.nn.sigmoid(acc + b), [(bg, 'row', 0)], out_dtype=F32,
                 name="in_gates")
    w, b = wb(o_m, o_m + 2 * D)
    gm = proj(w, lambda acc, b: jax.nn.sigmoid(acc + b), [(b, 'row', 0)], out_dtype=BF16,
              name="in_merge_gates")

    a_in = _rg_lru(u_lx, gelu_y, conv_w, conv_b, w_gate_a, b_gate_a, w_gate_x, b_gate_x,
                   lru_lambda, B=B, S=S)
    nc = S // CMP_STRIDE
    kc = _compress(kv_cmp, 0, cmp_pos_k, cmp_w1_k, cmp_w2_k, B=B, name="compress_k")
    vc = _compress(kv_cmp, 1, cmp_pos_v, cmp_w1_v, cmp_w2_v, B=B, name="compress_v")
    o_nsa = _nsa(q, kc, vc, kv4, gates, B=B, S=S)

    mixed = _merge(a_in, o_nsa, w_lru_out.astype(BF16), w_attn_out.astype(BF16), gm)
    x1 = _mm(mixed, w_o.astype(BF16), lambda acc, xr: xr + acc, [(x, 'tile', 0)],
             out_dtype=F32, name="out_proj")
    x2 = _mlp(x1, norm_mlp_g, w_up.astype(BF16), w_down.astype(BF16))
    return _ple(x2, p, norm_ple_g, w_ple_gate.astype(BF16), w_ple_proj.astype(BF16),
                norm_final_g, final_norm=final_norm)


def kernel(x, p, norm_mix_g, w_in, b_in, conv_w, conv_b, w_gate_a, b_gate_a, w_gate_x, b_gate_x, lru_lambda, w_lru_out, cmp_pos_k, cmp_w1_k, cmp_w2_k, cmp_pos_v, cmp_w1_v, cmp_w2_v, w_attn_out, w_o, norm_mlp_g, w_up, w_down, norm_ple_g, w_ple_gate, w_ple_proj, norm_final_g):
    B, S, D = x.shape
    depth = w_in.shape[0]
    assert S % 512 == 0 and S >= WINDOW + Q_TILE and D % 512 == 0
    xs = x.reshape(B * S, D)
    for i in range(depth):
        per_layer = [t[i] for t in (
            norm_mix_g, w_in, b_in, conv_w, conv_b, w_gate_a, b_gate_a, w_gate_x, b_gate_x,
            lru_lambda, w_lru_out, cmp_pos_k, cmp_w1_k, cmp_w2_k, cmp_pos_v, cmp_w1_v, cmp_w2_v,
            w_attn_out, w_o, norm_mlp_g, w_up, w_down, norm_ple_g, w_ple_gate, w_ple_proj)]
        xs = _layer(xs, p[i].reshape(B * S, -1), *per_layer, norm_final_g, B=B, S=S,
                    final_norm=i == depth - 1)
    return xs.reshape(B, S, D)
```

```python
import functools

import jax
import jax.numpy as jnp
from jax import lax
from jax.experimental import pallas as pl
from jax.experimental.pallas import tpu as pltpu

F32 = jnp.float32
BF16 = jnp.bfloat16

LRU_BLOCK_W = 128
CONV_WIDTH = 4
LRU_C = 8.0
N_HEADS = 16
N_KV_GROUPS = 4
HPG = N_HEADS // N_KV_GROUPS
HEAD_DIM = 128
CMP_STRIDE = 16
CMP_BLOCK = 32
SLC_BLOCK = 64
N_SELECT = 16
WINDOW = 512
Q_TILE = 256
EPS = 1e-6
NEG = -1e30
FORCED_SCORE = 1e4
KNOCKED_OUT = -3e38
NB = 128
LOG2E = 1.4426950408889634

VMEM_LIMIT = 56 * 1024 * 1024


def _params(sem):
    return pltpu.CompilerParams(dimension_semantics=sem, vmem_limit_bytes=VMEM_LIMIT)


def _rms(xf, g):
    return xf * lax.rsqrt(jnp.mean(xf * xf, axis=-1, keepdims=True) + EPS) * g


def _gelu(x):
    return jax.nn.gelu(x)


def _mm_body(*refs, n_extra, epi, split, pack, norm):
    a_ref, w_ref = refs[0], refs[1]
    extras = refs[2 + norm:2 + norm + n_extra]
    o_ref = refs[2 + norm + n_extra]
    if norm:
        h_ref = refs[3 + norm + n_extra]

        @pl.when(pl.program_id(1) == 0)
        def _():
            h_ref[...] = _rms(a_ref[...], refs[2][...]).astype(h_ref.dtype)

        lhs = h_ref[...]
    else:
        lhs = a_ref[...]
    acc = jnp.dot(lhs, w_ref[...], preferred_element_type=F32)
    res = epi(acc, *[e[...] for e in extras])
    if split and pack:
        stage_ref = refs[3 + norm + n_extra]
        rows = res.shape[0] // pack
        for g in range(split):
            stage_ref[...] = res[:, g * HEAD_DIM:(g + 1) * HEAD_DIM]
            for t in range(pack):
                o_ref[g, :, t * HEAD_DIM:(t + 1) * HEAD_DIM] = (
                    stage_ref[pl.ds(t, rows, stride=pack), :].astype(o_ref.dtype))
    elif split:
        for g in range(split):
            o_ref[g] = res[:, g * HEAD_DIM:(g + 1) * HEAD_DIM].astype(o_ref.dtype)
    else:
        o_ref[...] = res.astype(o_ref.dtype)


def _mm(a, w, epi, extras=(), *, out_dtype, tm=1024, tn=1024, split=0, pack=0, norm_g=None,
        name):
    M, K = a.shape
    N = w.shape[1]
    tn = min(tn, N)
    grid = (M // tm, N // tn)
    in_specs = [pl.BlockSpec((tm, K), lambda i, j: (i, 0)),
                pl.BlockSpec((K, tn), lambda i, j: (0, j))]
    args = [a, w]
    if norm_g is not None:
        in_specs.append(pl.BlockSpec((1, K), lambda i, j: (0, 0)))
        args.append(norm_g.reshape(1, K).astype(F32))
    for arr, kind, off in extras:
        if kind == 'row':
            in_specs.append(pl.BlockSpec((1, tn), lambda i, j, off=off: (0, j + off)))
        else:
            in_specs.append(pl.BlockSpec((tm, tn), lambda i, j, off=off: (i, j + off)))
        args.append(arr)
    scratch = []
    if split and pack:
        out_shape = jax.ShapeDtypeStruct((N // tn, split, M // pack, pack * HEAD_DIM), out_dtype)
        out_spec = pl.BlockSpec((None, split, tm // pack, pack * HEAD_DIM),
                                lambda i, j: (j, 0, i, 0))
        scratch = [pltpu.VMEM((tm, HEAD_DIM), F32)]
    elif split:
        out_shape = jax.ShapeDtypeStruct((N // tn, split, M, HEAD_DIM), out_dtype)
        out_spec = pl.BlockSpec((None, split, tm, HEAD_DIM), lambda i, j: (j, 0, i, 0))
    else:
        out_shape = jax.ShapeDtypeStruct((M, N), out_dtype)
        out_spec = pl.BlockSpec((tm, tn), lambda i, j: (i, j))
    if norm_g is not None:
        assert not split
        out_shape = [out_shape, jax.ShapeDtypeStruct((M, K), BF16)]
        out_spec = [out_spec, pl.BlockSpec((tm, K), lambda i, j: (i, 0))]
    return pl.pallas_call(
        functools.partial(_mm_body, n_extra=len(extras), epi=epi, split=split, pack=pack,
                          norm=int(norm_g is not None)),
        grid=grid, in_specs=in_specs, out_specs=out_spec, out_shape=out_shape,
        scratch_shapes=scratch,
        compiler_params=_params(("parallel", "arbitrary")),
        name=name,
    )(*args)


def _merge_body(a_ref, b_ref, wa_ref, wb_ref, ga_ref, gb_ref, o_ref):
    ya = jnp.dot(a_ref[...], wa_ref[...], preferred_element_type=F32)
    yb = jnp.dot(b_ref[...], wb_ref[...], preferred_element_type=F32)
    o_ref[...] = (ga_ref[...].astype(F32) * ya + gb_ref[...].astype(F32) * yb).astype(o_ref.dtype)


def _merge(a, b, wa, wb, gates, *, tm=1024, tn=512):
    M, K = a.shape
    N = wa.shape[1]
    nt = N // tn
    lhs = pl.BlockSpec((tm, K), lambda i, j: (i, 0))
    wsp = pl.BlockSpec((K, tn), lambda i, j: (0, j))
    return pl.pallas_call(
        _merge_body,
        grid=(M // tm, nt),
        in_specs=[lhs, lhs, wsp, wsp, pl.BlockSpec((tm, tn), lambda i, j: (i, j)),
                  pl.BlockSpec((tm, tn), lambda i, j: (i, j + nt))],
        out_specs=pl.BlockSpec((tm, tn), lambda i, j: (i, j)),
        out_shape=jax.ShapeDtypeStruct((M, N), BF16),
        compiler_params=_params(("parallel", "arbitrary")),
        name="mixer_merge",
    )(a, b, wa, wb, gates, gates)


def _lru_body(u_ref, gy_ref, cw_ref, cb_ref, wa_ref, ba_ref, wx_ref, bx_ref, lam_ref,
              o_ref, xe_ref, xc_ref, a_ref, b_ref, hn_ref, h_ref, *, T, ct):
    L = T // 8
    W = LRU_BLOCK_W
    nb = ct // W
    s = pl.program_id(2)

    @pl.when(s == 0)
    def _():
        xe_ref[:, 0:8, :] = jnp.zeros((nb, 8, W), F32)
        h_ref[...] = jnp.zeros((nb, 8, W), F32)

    lam = lam_ref[...]
    neg_softplus = -(jnp.maximum(-lam, 0.0) + jnp.log1p(jnp.exp(-jnp.abs(lam))))
    log_a_scale = LRU_C * neg_softplus
    for n in range(nb):
        sl = slice(n * W, (n + 1) * W)
        xe_ref[n, 8:, :] = u_ref[:, sl]

        def seg_rows(off):
            return xe_ref[n, pl.ds(off, 8, stride=L), :]

        taps = [jnp.broadcast_to(cw_ref[k:k + 1, sl], (8, W)) for k in range(CONV_WIDTH)]
        bias = jnp.broadcast_to(cb_ref[:, sl], (8, W))
        prev = [seg_rows(8 - CONV_WIDTH + 1 + k) for k in range(CONV_WIDTH - 1)]
        for t in range(L):
            win = prev + [seg_rows(8 + t)]
            y = bias
            for k in range(CONV_WIDTH):
                y = y + taps[k] * win[k]
            xc_ref[t * 8:(t + 1) * 8, :] = y
            prev = win[1:]
        xe_ref[n, 0:8, :] = xe_ref[n, T:T + 8, :]

        xc = xc_ref[...]
        xb = xc.astype(BF16)
        r = jax.nn.sigmoid(jnp.dot(xb, wa_ref[n], preferred_element_type=F32) + ba_ref[:, sl])
        i = jax.nn.sigmoid(jnp.dot(xb, wx_ref[n], preferred_element_type=F32) + bx_ref[:, sl])
        a = jnp.exp(r * log_a_scale[:, sl])
        a_ref[n] = a
        b_ref[n] = jnp.sqrt(jnp.maximum(1.0 - a * a, 0.0)) * (i * xc)

    def step(t, carry):
        rows = pl.ds(pl.multiple_of(t * 8, 8), 8)
        out = []
        for n in range(nb):
            h, decay = carry[n]
            a_t = a_ref[n, rows, :]
            h = a_t * h + b_ref[n, rows, :]
            decay = a_t * decay
            b_ref[n, rows, :] = h
            a_ref[n, rows, :] = decay
            out.append((h, decay))
        return tuple(out)

    init = tuple((jnp.zeros((8, W), F32), jnp.ones((8, W), F32)) for _ in range(nb))
    ends = lax.fori_loop(0, L, step, init, unroll=8)
    row = lax.broadcasted_iota(jnp.int32, (8, W), 0)
    for n in range(nb):
        h_end, a_end = ends[n]
        h_in = h_ref[n]
        for r in range(1, 8):
            h_in = jnp.where(row == r, pltpu.roll(a_end * h_in + h_end, 1, axis=0), h_in)
        h_ref[n] = jnp.broadcast_to((a_end * h_in + h_end)[7:8, :], (8, W))
        for t in range(L):
            rows = slice(t * 8, (t + 1) * 8)
            hn_ref[pl.ds(t, 8, stride=L), :] = b_ref[n, rows, :] + a_ref[n, rows, :] * h_in
        sl = slice(n * W, (n + 1) * W)
        o_ref[:, sl] = (hn_ref[...] * gy_ref[:, sl].astype(F32)).astype(o_ref.dtype)


def _rg_lru(u_lx, gelu_y, conv_w, conv_b, w_a, b_a, w_x, b_x, lam, *, B, S, T=512, ct=512):
    N, C = u_lx.shape
    nb = ct // LRU_BLOCK_W
    ns = S // T
    row = lambda v: v.reshape(1, C).astype(F32)
    tile = pl.BlockSpec((T, ct), lambda b, c, s: (b * ns + s, c))
    vec = pl.BlockSpec((1, ct), lambda b, c, s: (0, c))
    wblk = pl.BlockSpec((nb, LRU_BLOCK_W, LRU_BLOCK_W), lambda b, c, s: (c, 0, 0))
    return pl.pallas_call(
        functools.partial(_lru_body, T=T, ct=ct),
        grid=(B, C // ct, ns),
        in_specs=[tile, tile, pl.BlockSpec((CONV_WIDTH, ct), lambda b, c, s: (0, c)), vec,
                  wblk, vec, wblk, vec, vec],
        out_specs=tile,
        out_shape=jax.ShapeDtypeStruct((N, C), BF16),
        scratch_shapes=[pltpu.VMEM((nb, T + 8, LRU_BLOCK_W), F32), pltpu.VMEM((T, LRU_BLOCK_W), F32),
                        pltpu.VMEM((nb, T, LRU_BLOCK_W), F32), pltpu.VMEM((nb, T, LRU_BLOCK_W), F32),
                        pltpu.VMEM((T, LRU_BLOCK_W), F32), pltpu.VMEM((nb, 8, LRU_BLOCK_W), F32)],
        compiler_params=_params(("parallel", "parallel", "arbitrary")),
        name="rg_lru",
    )(u_lx, gelu_y, conv_w.astype(F32), row(conv_b), w_a.astype(BF16), row(b_a),
      w_x.astype(BF16), row(b_x), row(lam))


def _cmp_body(x_ref, pos_ref, w1_ref, w2_ref, o_ref, *, nc, half):
    x = x_ref[...]
    first = jnp.dot(x, w1_ref[0:half, :], preferred_element_type=F32)
    second = jnp.dot(x, w1_ref[half:, :], preferred_element_type=F32)
    pos = jnp.broadcast_to(pos_ref[...], (8, 2 * half)).astype(BF16)
    cpos = jnp.dot(pos, w1_ref[...], preferred_element_type=F32)[0:1, :]
    hid = first + pltpu.roll(second, nc - 1, axis=0) + cpos
    o_ref[...] = jnp.dot(_gelu(hid).astype(BF16), w2_ref[...],
                         preferred_element_type=F32).astype(o_ref.dtype)


def _compress(x, t, pos, w1, w2, *, B, name):
    _, G, rows, half = x.shape
    nc = rows // B
    hidden = w1.shape[1]
    return pl.pallas_call(
        functools.partial(_cmp_body, nc=nc, half=half),
        grid=(B, G),
        in_specs=[pl.BlockSpec((None, None, nc, half), lambda b, g: (t, g, b, 0)),
                  pl.BlockSpec((1, 2 * half), lambda b, g: (0, 0)),
                  pl.BlockSpec((2 * half, hidden), lambda b, g: (0, 0)),
                  pl.BlockSpec((hidden, HEAD_DIM), lambda b, g: (0, 0))],
        out_specs=pl.BlockSpec((None, None, nc, HEAD_DIM), lambda b, g: (b, g, 0, 0)),
        out_shape=jax.ShapeDtypeStruct((B, G, nc, HEAD_DIM), BF16),
        compiler_params=_params(("parallel", "parallel")),
        name=name,
    )(x, pos.reshape(1, 2 * half).astype(F32), w1.astype(BF16), w2.astype(BF16))


def _qk(q, k):
    return lax.dot_general(q, k, (((1,), (1,)), ((), ())), preferred_element_type=F32)


def _nsa_body(q_ref, kc_ref, vc_ref, ks_ref, vs_ref, kw_ref, vw_ref, g_ref, o_ref,
              kaug_ref, vaug_ref, vwaug_ref, ovt_ref, qaug_ref, m_ref, acc_ref,
              sbuf_ref, part_ref, *, S, nc, n_sb, n_sel, tk):
    R = HPG * Q_TILE
    D2 = 2 * HEAD_DIM
    i = pl.program_id(2)
    s0 = i * Q_TILE

    @pl.when(i == 0)
    def _():
        key_blk = lax.broadcasted_iota(jnp.int32, (S, NB), 0) // SLC_BLOCK
        col = lax.broadcasted_iota(jnp.int32, (S, NB), 1)
        kaug_ref[:, 0:HEAD_DIM] = ks_ref[...]
        kaug_ref[:, HEAD_DIM:D2] = (key_blk == col).astype(BF16)
        vaug_ref[:, 0:HEAD_DIM] = vs_ref[...]
        vaug_ref[:, HEAD_DIM:D2] = jnp.ones((S, HEAD_DIM), BF16)
        vwaug_ref[:, 0:HEAD_DIM] = vw_ref[...]
        vwaug_ref[:, HEAD_DIM:D2] = jnp.ones((S, HEAD_DIM), BF16)
        ci = lax.broadcasted_iota(jnp.int32, (NB, nc), 1) * CMP_STRIDE
        si = lax.broadcasted_iota(jnp.int32, (NB, nc), 0) * SLC_BLOCK
        ovt_ref[...] = ((ci < si + SLC_BLOCK) & (si < ci + CMP_BLOCK)).astype(BF16)

    qb = q_ref[...]
    q4 = jnp.concatenate([qb[:, h * HEAD_DIM:(h + 1) * HEAD_DIM] for h in range(HPG)], axis=0)
    tq = s0 + lax.broadcasted_iota(jnp.int32, (Q_TILE, 1), 0)

    def softmax_pv(s, bias, v_aug):
        k = s.shape[-1]
        s3 = s.reshape(HPG, Q_TILE, k) + bias[None]
        e = jnp.exp2(s3 - jnp.max(s3, axis=-1, keepdims=True))
        return e, jnp.dot(e.reshape(R, k).astype(BF16), v_aug, preferred_element_type=F32)

    cend = lax.broadcasted_iota(jnp.int32, (1, nc), 1) * CMP_STRIDE + (CMP_BLOCK - 1)
    s_c = _qk(q4, kc_ref[...]).reshape(HPG, Q_TILE, nc) + jnp.where(cend <= tq, 0.0, NEG)[None]
    e_c = jnp.exp2(s_c - jnp.max(s_c, axis=-1, keepdims=True))
    inv_c = 1.0 / jnp.maximum(jnp.sum(e_c, axis=-1, keepdims=True), 1e-30)

    pc_sum = e_c[0] * inv_c[0]
    for h in range(1, HPG):
        pc_sum = pc_sum + e_c[h] * inv_c[h]
    pc_hi = pc_sum.astype(BF16)
    pc_lo = (pc_sum - pc_hi.astype(F32)).astype(BF16)
    ovt = ovt_ref[...]
    imp = _qk(ovt, pc_hi) + _qk(ovt, pc_lo)
    blk = lax.broadcasted_iota(jnp.int32, (NB, 1), 0)
    cur = (s0 + lax.broadcasted_iota(jnp.int32, (1, Q_TILE), 1)) // SLC_BLOCK
    forced = (blk == 0) | (blk == cur) | (blk == cur - 1)
    score = jnp.where(forced, KNOCKED_OUT, jnp.where(blk <= cur, imp, -FORCED_SCORE))
    if n_sb < NB:
        score = jnp.where(blk < n_sb, score, KNOCKED_OUT)
    blk_f = blk.astype(F32)
    sel = forced.astype(F32)
    def pick(score, sel):
        mx = jnp.max(score, axis=0, keepdims=True)
        first = jnp.min(jnp.where(score == mx, blk_f, float(NB)), axis=0, keepdims=True)
        hit = blk_f == first
        return jnp.where(hit, KNOCKED_OUT, score), jnp.where(hit, 1.0, sel)

    for _ in range((n_sel - 3) // 2):
        score, sel = pick(score, sel)

    oc = jnp.dot(e_c.reshape(R, nc).astype(BF16), vc_ref[...], preferred_element_type=F32)
    any_c = (tq >= CMP_BLOCK - 1).astype(F32)
    o_cmp = oc.reshape(HPG, Q_TILE, HEAD_DIM) * (inv_c * any_c[None])
    span = WINDOW + Q_TILE
    w0 = pl.multiple_of(jnp.maximum(s0 - WINDOW, 0), Q_TILE)
    diff = tq - (w0 + lax.broadcasted_iota(jnp.int32, (1, span), 1))
    _, ow = softmax_pv(_qk(q4, kw_ref[pl.ds(w0, span), :]),
                       jnp.where((diff >= 0) & (diff < WINDOW), 0.0, NEG),
                       vwaug_ref[pl.ds(w0, span), :])
    o_win = (ow[:, :HEAD_DIM] / jnp.maximum(ow[:, HEAD_DIM:], 1e-30)
             ).reshape(HPG, Q_TILE, HEAD_DIM)
    gt = g_ref[...]
    for h in range(HPG):
        part_ref[:, h * HEAD_DIM:(h + 1) * HEAD_DIM] = (
            gt[:, h:h + 1] * o_cmp[h] + gt[:, 2 * HPG + h:2 * HPG + h + 1] * o_win[h])

    for _ in range((n_sel - 3) - (n_sel - 3) // 2):
        score, sel = pick(score, sel)

    def mask_columns(sel_nq):
        bias = ((sel_nq - 1.0) * (-NEG)).T.astype(BF16)
        return jnp.concatenate([bias] * HPG, axis=0)

    qaug_ref[:, 0:HEAD_DIM] = q4
    qaug_ref[:, HEAD_DIM:D2] = mask_columns(jnp.where(blk < s0 // SLC_BLOCK, sel, 0.0))

    own = pl.ds(pl.multiple_of(s0, Q_TILE), Q_TILE)
    lower = (lax.broadcasted_iota(jnp.int32, (Q_TILE, Q_TILE), 1)
             <= lax.broadcasted_iota(jnp.int32, (Q_TILE, Q_TILE), 0))
    q_own = jnp.concatenate([q4, mask_columns(sel)], axis=1)
    s_own = (_qk(q_own, kaug_ref[own, :]).reshape(HPG, Q_TILE, Q_TILE)
             + jnp.where(lower, 0.0, NEG)[None]).reshape(R, Q_TILE)
    m_own = jnp.broadcast_to(jnp.max(s_own, axis=-1, keepdims=True), (R, HEAD_DIM))
    m_ref[...] = m_own
    p_own = jnp.exp2(s_own - jnp.concatenate([m_own] * (Q_TILE // HEAD_DIM), axis=1))
    acc_ref[...] = jnp.dot(p_own.astype(BF16), vaug_ref[own, :],
                           preferred_element_type=F32)
    n_past = (s0 + tk - 1) // tk

    def scores(kt):
        return _qk(qaug_ref[...], kaug_ref[pl.ds(pl.multiple_of(kt * tk, tk), tk), :])

    def consume(kt):
        s = sbuf_ref[...]
        m_old = m_ref[...]
        m_new = jnp.maximum(m_old, jnp.max(s, axis=-1, keepdims=True))
        p = jnp.exp2(s - jnp.concatenate([m_new] * (tk // HEAD_DIM), axis=1)).astype(BF16)
        alpha = jnp.exp2(m_old - m_new)
        m_ref[...] = m_new
        acc_ref[...] = (jnp.concatenate([alpha, alpha], axis=1) * acc_ref[...]
                        + jnp.dot(p, vaug_ref[pl.ds(pl.multiple_of(kt * tk, tk), tk), :],
                                  preferred_element_type=F32))

    sbuf_ref[...] = scores(0)

    @pl.loop(0, jnp.maximum(n_past - 1, 0))
    def _(kt):
        s_next = scores(kt + 1)
        consume(kt)
        sbuf_ref[...] = s_next

    consume(jnp.maximum(n_past - 1, 0))
    acc_s = acc_ref[...]
    o_slc = (acc_s[:, :HEAD_DIM] / jnp.maximum(acc_s[:, HEAD_DIM:], 1e-30)
             ).reshape(HPG, Q_TILE, HEAD_DIM)

    gt = g_ref[...]
    for h in range(HPG):
        cols = slice(h * HEAD_DIM, (h + 1) * HEAD_DIM)
        o_ref[:, cols] = (part_ref[:, cols]
                          + gt[:, HPG + h:HPG + h + 1] * o_slc[h]).astype(o_ref.dtype)


def _nsa(q, kc, vc, kv4, gates, *, B, S, tk=1024):
    N = B * S
    nq = S // Q_TILE
    nc = kc.shape[2]
    n_sb = S // SLC_BLOCK
    assert n_sb <= NB and nc % HEAD_DIM == 0 and S % tk == 0
    G = N_KV_GROUPS
    gw = HPG * HEAD_DIM
    kv_spec = lambda t: pl.BlockSpec((None, None, S, HEAD_DIM), lambda b, g, i, t=t: (t, g, b, 0))
    cmp_spec = pl.BlockSpec((None, None, nc, HEAD_DIM), lambda b, g, i: (b, g, 0, 0))
    return pl.pallas_call(
        functools.partial(_nsa_body, S=S, nc=nc, n_sb=n_sb, n_sel=min(N_SELECT, n_sb), tk=tk),
        grid=(B, G, nq),
        in_specs=[pl.BlockSpec((Q_TILE, gw), lambda b, g, i: (b * nq + i, g)),
                  cmp_spec, cmp_spec, kv_spec(0), kv_spec(1), kv_spec(2), kv_spec(3),
                  pl.BlockSpec((Q_TILE, 128), lambda b, g, i: (b * nq + i, g))],
        out_specs=pl.BlockSpec((Q_TILE, gw), lambda b, g, i: (b * nq + i, g)),
        out_shape=jax.ShapeDtypeStruct((N, G * gw), BF16),
        scratch_shapes=[pltpu.VMEM((S, 2 * HEAD_DIM), BF16), pltpu.VMEM((S, 2 * HEAD_DIM), BF16),
                        pltpu.VMEM((S, 2 * HEAD_DIM), BF16),
                        pltpu.VMEM((NB, nc), BF16),
                        pltpu.VMEM((HPG * Q_TILE, 2 * HEAD_DIM), BF16),
                        pltpu.VMEM((HPG * Q_TILE, HEAD_DIM), F32),
                        pltpu.VMEM((HPG * Q_TILE, 2 * HEAD_DIM), F32),
                        pltpu.VMEM((HPG * Q_TILE, tk), F32),
                        pltpu.VMEM((Q_TILE, HPG * HEAD_DIM), F32)],
        compiler_params=_params(("parallel", "parallel", "arbitrary")),
        name="nsa_attention",
    )(q, kc, vc, kv4, kv4, kv4, kv4, gates)


def _mlp_body(x_ref, g_ref, wu_ref, wd_ref, o_ref, h_ref, acc_ref):
    f = pl.program_id(1)

    @pl.when(f == 0)
    def _():
        h_ref[...] = _rms(x_ref[...], g_ref[...]).astype(BF16)
        acc_ref[...] = jnp.zeros_like(acc_ref)

    up = jnp.dot(h_ref[...], wu_ref[...], preferred_element_type=F32)
    up = jnp.square(jnp.maximum(up, 0.0)).astype(BF16)
    acc_ref[...] += jnp.dot(up, wd_ref[...], preferred_element_type=F32)

    @pl.when(f == pl.num_programs(1) - 1)
    def _():
        o_ref[...] = x_ref[...] + acc_ref[...]


def _mlp(x, g, w_up, w_down, *, tm=512, tf=1024):
    M, D = x.shape
    FF = w_up.shape[1]
    return pl.pallas_call(
        _mlp_body,
        grid=(M // tm, FF // tf),
        in_specs=[pl.BlockSpec((tm, D), lambda i, f: (i, 0)),
                  pl.BlockSpec((1, D), lambda i, f: (0, 0)),
                  pl.BlockSpec((D, tf), lambda i, f: (0, f)),
                  pl.BlockSpec((tf, D), lambda i, f: (f, 0))],
        out_specs=pl.BlockSpec((tm, D), lambda i, f: (i, 0)),
        out_shape=jax.ShapeDtypeStruct((M, D), F32),
        scratch_shapes=[pltpu.VMEM((tm, D), BF16), pltpu.VMEM((tm, D), F32)],
        compiler_params=_params(("parallel", "arbitrary")),
        name="mlp",
    )(x, g.reshape(1, D).astype(F32), w_up, w_down)


def _ple_body(x_ref, p_ref, g_ref, wg_ref, wp_ref, gf_ref, o_ref, *, final_norm):
    x = x_ref[...]
    h = _rms(x, g_ref[...]).astype(BF16)
    gate = jax.nn.sigmoid(jnp.dot(h, wg_ref[...], preferred_element_type=F32))
    proj = jnp.dot(p_ref[...].astype(BF16), wp_ref[...], preferred_element_type=F32)
    y = x + gate * proj
    o_ref[...] = _rms(y, gf_ref[...]) if final_norm else y


def _ple(x, p, g, w_gate, w_proj, g_final, *, final_norm, tm=512):
    M, D = x.shape
    P = p.shape[1]
    row = lambda v: v.reshape(1, D).astype(F32)
    vec = pl.BlockSpec((1, D), lambda i: (0, 0))
    return pl.pallas_call(
        functools.partial(_ple_body, final_norm=final_norm),
        grid=(M // tm,),
        in_specs=[pl.BlockSpec((tm, D), lambda i: (i, 0)), pl.BlockSpec((tm, P), lambda i: (i, 0)),
                  vec, pl.BlockSpec((D, D), lambda i: (0, 0)), pl.BlockSpec((P, D), lambda i: (0, 0)),
                  vec],
        out_specs=pl.BlockSpec((tm, D), lambda i: (i, 0)),
        out_shape=jax.ShapeDtypeStruct((M, D), F32),
        compiler_params=_params(("parallel",)),
        name="ple_final",
    )(x, p, row(g), w_gate, w_proj, row(g_final))


def _layer(x, p, norm_mix_g, w_in, b_in, conv_w, conv_b, w_gate_a, b_gate_a, w_gate_x, b_gate_x,
           lru_lambda, w_lru_out, cmp_pos_k, cmp_w1_k, cmp_w2_k, cmp_pos_v, cmp_w1_v, cmp_w2_v,
           w_attn_out, w_o, norm_mlp_g, w_up, w_down, norm_ple_g, w_ple_gate, w_ple_proj,
           norm_final_g, *, B, S, final_norm):
    N, D = x.shape
    C = conv_w.shape[1]
    G = N_KV_GROUPS
    QW = N_HEADS * HEAD_DIM
    KVW = G * HEAD_DIM
    o_ly, o_q, o_kv, o_g, o_m = C, 2 * C, 2 * C + QW, 2 * C + QW + 6 * KVW, 2 * C + QW + 6 * KVW + 3 * N_HEADS
    wb = lambda lo, hi: (w_in[:, lo:hi].astype(BF16), b_in[lo:hi].reshape(1, hi - lo).astype(F32))
    w, b = wb(0, o_ly)
    u_lx, h = _mm(x, w, lambda acc, b: acc + b, [(b, 'row', 0)], out_dtype=F32, norm_g=norm_mix_g,
                  name="in_lru_x")
    proj = functools.partial(_mm, h)
    w, b = wb(o_ly, o_q)
    gelu_y = proj(w, lambda acc, b: _gelu(acc + b), [(b, 'row', 0)], out_dtype=BF16, name="in_lru_y")
    w, b = wb(o_q, o_kv)
    q = proj(w, lambda acc, b: (acc + b) * (HEAD_DIM ** -0.5 * LOG2E), [(b, 'row', 0)],
             out_dtype=BF16, name="in_q")
    w, b = wb(o_kv, o_kv + 2 * KVW)
    kv_cmp = proj(w, lambda acc, b: acc + b, [(b, 'row', 0)], out_dtype=BF16, tn=KVW, split=G,
                  pack=CMP_STRIDE, name="in_kv_cmp")
    w, b = wb(o_kv + 2 * KVW, o_g)
    kv4 = proj(w, lambda acc, b: acc + b, [(b, 'row', 0)], out_dtype=BF16, tn=KVW, split=G,
               name="in_kv")
    wg = w_in[:, o_g:o_m].reshape(D, 3, G, HPG).transpose(0, 2, 1, 3).reshape(D, G, 3 * HPG)
    wg = jnp.pad(wg, ((0, 0), (0, 0), (0, 128 - 3 * HPG))).reshape(D, G * 128).astype(BF16)
    bg = b_in[o_g:o_m].reshape(3, G, HPG).transpose(1, 0, 2).reshape(G, 3 * HPG)
    bg = jnp.pad(bg, ((0, 0), (0, 128 - 3 * HPG))).reshape(1, G * 128).astype(F32)
    gates = proj(wg, lambda acc, b: jax.nn.sigmoid(acc + b), [(bg, 'row', 0)], out_dtype=F32,
                 name="in_gates")
    w, b = wb(o_m, o_m + 2 * D)
    gm = proj(w, lambda acc, b: jax.nn.sigmoid(acc + b), [(b, 'row', 0)], out_dtype=BF16,
              name="in_merge_gates")

    a_in = _rg_lru(u_lx, gelu_y, conv_w, conv_b, w_gate_a, b_gate_a, w_gate_x, b_gate_x,
                   lru_lambda, B=B, S=S)
    nc = S // CMP_STRIDE
    kc = _compress(kv_cmp, 0, cmp_pos_k, cmp_w1_k, cmp_w2_k, B=B, name="compress_k")
    vc = _compress(kv_cmp, 1, cmp_pos_v, cmp_w1_v, cmp_w2_v, B=B, name="compress_v")
    o_nsa = _nsa(q, kc, vc, kv4, gates, B=B, S=S)

    mixed = _merge(a_in, o_nsa, w_lru_out.astype(BF16), w_attn_out.astype(BF16), gm)
    x1 = _mm(mixed, w_o.astype(BF16), lambda acc, xr: xr + acc, [(x, 'tile', 0)],
             out_dtype=F32, name="out_proj")
    x2 = _mlp(x1, norm_mlp_g, w_up.astype(BF16), w_down.astype(BF16))
    return _ple(x2, p, norm_ple_g, w_ple_gate.astype(BF16), w_ple_proj.astype(BF16),
                norm_final_g, final_norm=final_norm)


def kernel(x, p, norm_mix_g, w_in, b_in, conv_w, conv_b, w_gate_a, b_gate_a, w_gate_x, b_gate_x, lru_lambda, w_lru_out, cmp_pos_k, cmp_w1_k, cmp_w2_k, cmp_pos_v, cmp_w1_v, cmp_w2_v, w_attn_out, w_o, norm_mlp_g, w_up, w_down, norm_ple_g, w_ple_gate, w_ple_proj, norm_final_g):
    B, S, D = x.shape
    depth = w_in.shape[0]
    assert S % 512 == 0 and S >= WINDOW + Q_TILE and D % 512 == 0
    xs = x.reshape(B * S, D)
    for i in range(depth):
        per_layer = [t[i] for t in (
            norm_mix_g, w_in, b_in, conv_w, conv_b, w_gate_a, b_gate_a, w_gate_x, b_gate_x,
            lru_lambda, w_lru_out, cmp_pos_k, cmp_w1_k, cmp_w2_k, cmp_pos_v, cmp_w1_v, cmp_w2_v,
            w_attn_out, w_o, norm_mlp_g, w_up, w_down, norm_ple_g, w_ple_gate, w_ple_proj)]
        xs = _layer(xs, p[i].reshape(B * S, -1), *per_layer, norm_final_g, B=B, S=S,
                    final_norm=i == depth - 1)
    return xs.reshape(B, S, D)
```
